```python
import math
import jax, jax.numpy as jnp
from jax import lax
import numpy as np

D_MODEL = 2048
BATCH = 4
SEQ = 2048
DEPTH = 2

N_META = 16
GRID_W = 64
EPS = 1e-5
N_EVEN = (DEPTH + 1) // 2
N_ODD = DEPTH // 2

SSD_HEAD_DIM = 64
SSD_WIDTH = D_MODEL
SSD_HEADS = SSD_WIDTH // SSD_HEAD_DIM
SSD_GROUPS = 8
SSD_STATE = 128
SSD_GN = SSD_GROUPS * SSD_STATE
SSD_CONV = 5
SSD_CONV_CH = SSD_WIDTH + 2 * SSD_GN
CHUNK = 128

CONV_WIDTH = D_MODEL
CONV_KERNEL = 31

IN_E_SIZES = (SSD_WIDTH,
              SSD_WIDTH,
              SSD_GN,
              SSD_GN,
              SSD_HEADS,
              SSD_HEADS,
              CONV_WIDTH,
              CONV_WIDTH,
              CONV_WIDTH)
IN_E = sum(IN_E_SIZES)
OUT_E = SSD_WIDTH + CONV_WIDTH

NA_HEAD_DIM = 64
NA_WIDTH = D_MODEL
NA_HEADS = NA_WIDTH // NA_HEAD_DIM
NA_KH = 8
NA_KW = 16
IN_O = 4 * NA_WIDTH

kernel_name = "hybrid_ssd_conformer_natten_encoder"


def split_cols(a, sizes):
    idx = [int(i) for i in np.cumsum(sizes)[:-1]]
    return jnp.split(a, idx, axis=-1)


def rmsnorm(x, g):
    xf = x.astype(jnp.float32)
    y = xf * lax.rsqrt(jnp.mean(xf * xf, axis=-1, keepdims=True) + EPS)
    return (y * g.astype(jnp.float32)).astype(x.dtype)


def layernorm(x, g, b):
    xf = x.astype(jnp.float32)
    mu = jnp.mean(xf, axis=-1, keepdims=True)
    var = jnp.mean(jnp.square(xf - mu), axis=-1, keepdims=True)
    y = (xf - mu) * lax.rsqrt(var + EPS)
    return (y * g.astype(jnp.float32) + b.astype(jnp.float32)).astype(x.dtype)


def depthwise_conv(x, w, b):
    y = lax.conv_general_dilated(
        x, w[:, None, :].astype(x.dtype), window_strides=(1,), padding="SAME",
        dimension_numbers=("NWC", "WIO", "NWC"), feature_group_count=x.shape[-1])
    return y + b.astype(x.dtype)


def ssd_scan(xdt, log_a, Bm, Cm):
    b, T, h, p = xdt.shape
    g, n = Bm.shape[2], Bm.shape[3]
    r = h // g
    c = T // CHUNK
    X = xdt.reshape(b, c, CHUNK, g, r, p)
    A = log_a.reshape(b, c, CHUNK, g, r).astype(jnp.float32)
    Bc = Bm.reshape(b, c, CHUNK, g, n)
    Cc = Cm.reshape(b, c, CHUNK, g, n)
    A_cs = jnp.cumsum(A, axis=2)
    tri = jnp.tril(jnp.ones((CHUNK, CHUNK), dtype=bool))[None, None, :, :, None, None]
    seg = A_cs[:, :, :, None] - A_cs[:, :, None, :]
    Lmat = jnp.exp(jnp.where(tri, seg, -jnp.inf))
    CB = jnp.einsum("bclgn,bcsgn->bclsg", Cc, Bc)
    Y_diag = jnp.einsum("bclsgr,bcsgrp->bclgrp", CB[..., None] * Lmat, X)
    Xd = X * jnp.exp(A_cs[:, :, -1:] - A_cs)[..., None]
    states = jnp.einsum("bclgn,bclgrp->bcgrpn", Bc, Xd)
    chunk_decay = jnp.exp(A_cs[:, :, -1])

    def step(carry, inp):
        st, dec = inp
        return carry * dec[..., None, None] + st, carry

    init = jnp.zeros((b, g, r, p, n), dtype=states.dtype)
    _, prev = lax.scan(step, init, (jnp.moveaxis(states, 1, 0), jnp.moveaxis(chunk_decay, 1, 0)))
    prev = jnp.moveaxis(prev, 0, 1)
    Y_off = jnp.einsum("bclgn,bcgrpn->bclgrp", Cc, prev) * jnp.exp(A_cs)[..., None]
    return (Y_diag + Y_off).reshape(b, T, h, p)


def ssd_mixer(xs, Bm, Cm, dt_f, dt_b, z, conv_w, conv_b, dt_bias, A_log, D_skip, norm_g):
    b, L, _ = xs.shape
    xbc = jax.nn.silu(depthwise_conv(jnp.concatenate([xs, Bm, Cm], axis=-1), conv_w, conv_b))
    xs, Bm, Cm = split_cols(xbc, (SSD_WIDTH, SSD_GN, SSD_GN))
    xh = xs.reshape(b, L, SSD_HEADS, SSD_HEAD_DIM)
    Bg = Bm.reshape(b, L, SSD_GROUPS, SSD_STATE)
    Cg = Cm.reshape(b, L, SSD_GROUPS, SSD_STATE)
    pad = (-L) % CHUNK

    def padl(a):
        return jnp.pad(a, [(0, 0), (pad, 0)] + [(0, 0)] * (a.ndim - 2))

    Bp, Cp = padl(Bg), padl(Cg)

    def direction(dt_raw, bias, a_log, reverse):
        dt = jax.nn.softplus(dt_raw.astype(jnp.float32) + bias.astype(jnp.float32))
        log_a = dt * (-jnp.exp(a_log.astype(jnp.float32)))
        xdt = xh.astype(jnp.float32) * dt[..., None]
        args = [padl(xdt), padl(log_a), Bp, Cp]
        if reverse:
            args = [jnp.flip(a, axis=1) for a in args]
        y = ssd_scan(*args)
        if reverse:
            y = jnp.flip(y, axis=1)
        return y[:, pad:]

    y = (direction(dt_f, dt_bias[0], A_log[0], False)
         + direction(dt_b, dt_bias[1], A_log[1], True)
         + D_skip.astype(jnp.float32)[:, None] * xh.astype(jnp.float32))
    y = y.reshape(b, L, SSD_WIDTH).astype(xs.dtype)
    return rmsnorm(y * jax.nn.silu(z), norm_g)


def conformer_conv(val, gate, dw_w, dw_b, ln_g, ln_b):
    u = val * jax.nn.sigmoid(gate)
    u = depthwise_conv(u, dw_w, dw_b)
    return jax.nn.silu(layernorm(u, ln_g, ln_b))


def neighbourhood_attention(q, k, v, rpb):
    b, L, h, d = q.shape
    S = L - N_META
    rows = S // GRID_W
    kh = min(NA_KH, rows)
    kw = NA_KW
    scale = d ** -0.5
    qm, km, vm = q[:, :N_META], k[:, :N_META], v[:, :N_META]
    qg = q[:, N_META:].reshape(b, rows, GRID_W, h, d)
    kg = k[:, N_META:].reshape(b, rows, GRID_W, h, d)
    vg = v[:, N_META:].reshape(b, rows, GRID_W, h, d)

    s_mm = jnp.einsum("bqhd,bkhd->bhqk", qm, km).astype(jnp.float32) * scale
    o_meta = jnp.einsum("bhqk,bkhd->bqhd", jax.nn.softmax(s_mm, axis=-1).astype(v.dtype), vm)

    col = np.arange(GRID_W)
    col_start = np.clip(col - kw // 2, 0, GRID_W - kw)
    col_mask = (col[None, :] >= col_start[:, None]) & (col[None, :] < col_start[:, None] + kw)
    col_idx = np.clip(col[None, :] - col[:, None] + NA_KW - 1, 0, 2 * NA_KW - 2)
    bias_cols = rpb[:, :, col_idx].astype(jnp.float32)

    def row_block(r):
        rs = jnp.clip(r - kh // 2, 0, rows - kh)
        k_rows = lax.dynamic_slice_in_dim(kg, rs, kh, axis=1)
        v_rows = lax.dynamic_slice_in_dim(vg, rs, kh, axis=1)
        q_row = lax.dynamic_index_in_dim(qg, r, axis=1, keepdims=False)
        s = jnp.einsum("bqhd,bjkhd->bhqjk", q_row, k_rows).astype(jnp.float32) * scale
        dr = rs + jnp.arange(kh) - r + NA_KH - 1
        bias = jnp.take(bias_cols, dr, axis=1)
        s = s + jnp.transpose(bias, (0, 2, 1, 3))[None]
        s = jnp.where(col_mask[None, None, :, None, :], s, -jnp.inf)
        s = s.reshape(b, h, GRID_W, kh * GRID_W)
        s_m = jnp.einsum("bqhd,bkhd->bhqk", q_row, km).astype(jnp.float32) * scale
        p = jax.nn.softmax(jnp.concatenate([s, s_m], axis=-1), axis=-1).astype(v.dtype)
        o = (jnp.einsum("bhqj,bjhd->bqhd", p[..., :kh * GRID_W], v_rows.reshape(b, kh * GRID_W, h, d))
             + jnp.einsum("bhqm,bmhd->bqhd", p[..., kh * GRID_W:], vm))
        return o

    o_grid = lax.map(row_block, jnp.arange(rows))
    o_grid = jnp.moveaxis(o_grid, 0, 1).reshape(b, S, h, d)
    return jnp.concatenate([o_meta, o_grid], axis=1)


def even_layer(h, norm_g, w_in, conv_w, conv_b, dt_bias, A_log, D_skip, ssd_norm_g,
               dw_w, dw_b, ln_g, ln_b, w_out):
    u = rmsnorm(h, norm_g)
    proj = jnp.einsum("bld,de->ble", u, w_in)
    z_a, xs, Bm, Cm, dt_f, dt_b, z_b, glu_v, glu_g = split_cols(proj, IN_E_SIZES)
    y_a = ssd_mixer(xs, Bm, Cm, dt_f, dt_b, z_a, conv_w, conv_b, dt_bias, A_log, D_skip, ssd_norm_g)
    y_b = conformer_conv(glu_v, glu_g, dw_w, dw_b, ln_g, ln_b) * jax.nn.silu(z_b)
    return h + jnp.einsum("ble,ed->bld", jnp.concatenate([y_a, y_b], axis=-1), w_out)


def odd_layer(h, norm_g, w_in, rpb, w_out):
    b, L, _ = h.shape
    u = rmsnorm(h, norm_g)
    proj = jnp.einsum("bld,de->ble", u, w_in)
    q, k, v, z = split_cols(proj, (NA_WIDTH,) * 4)
    shp = (b, L, NA_HEADS, NA_HEAD_DIM)
    o = neighbourhood_attention(q.reshape(shp), k.reshape(shp), v.reshape(shp), rpb)
    o = o.reshape(b, L, NA_WIDTH) * jax.nn.silu(z)
    return h + jnp.einsum("ble,ed->bld", o, w_out)


def setup_inputs(seed: int = 0) -> dict:
    key = jax.random.key(seed)
    ks = jax.random.split(key, 24)
    f32 = jnp.float32
    ne, no = N_EVEN, N_ODD

    def nrm(k, shape, scale):
        return jax.random.normal(k, shape, f32) * scale

    dt0 = jnp.exp(jax.random.uniform(ks[6], (ne, 2, SSD_HEADS), f32, math.log(1e-3), math.log(1e-1)))
    return {
        "x": nrm(ks[0], (BATCH, SEQ, D_MODEL), 1.0),
        "meta_tokens": nrm(ks[1], (N_META, D_MODEL), 1.0),
        "e_norm_g": 1.0 + nrm(ks[2], (ne, D_MODEL), 0.02),
        "e_w_in": nrm(ks[3], (ne, D_MODEL, IN_E), D_MODEL ** -0.5),
        "e_conv_w": nrm(ks[4], (ne, SSD_CONV, SSD_CONV_CH), SSD_CONV ** -0.5),
        "e_conv_b": nrm(ks[5], (ne, SSD_CONV_CH), 0.02),
        "e_dt_bias": dt0 + jnp.log(-jnp.expm1(-dt0)),
        "e_A_log": jnp.log(jax.random.uniform(ks[7], (ne, 2, SSD_HEADS), f32, 1.0, 16.0)),
        "e_D": 1.0 + nrm(ks[8], (ne, SSD_HEADS), 0.1),
        "e_ssd_norm_g": 1.0 + nrm(ks[9], (ne, SSD_WIDTH), 0.02),
        "e_dw_w": nrm(ks[10], (ne, CONV_KERNEL, CONV_WIDTH), CONV_KERNEL ** -0.5),
        "e_dw_b": nrm(ks[11], (ne, CONV_WIDTH), 0.02),
        "e_ln_g": 1.0 + nrm(ks[12], (ne, CONV_WIDTH), 0.02),
        "e_ln_b": nrm(ks[13], (ne, CONV_WIDTH), 0.02),
        "e_w_out": nrm(ks[14], (ne, OUT_E, D_MODEL), OUT_E ** -0.5),
        "o_norm_g": 1.0 + nrm(ks[15], (no, D_MODEL), 0.02),
        "o_w_in": nrm(ks[16], (no, D_MODEL, IN_O), D_MODEL ** -0.5),
        "o_rpb": nrm(ks[17], (no, NA_HEADS, 2 * NA_KH - 1, 2 * NA_KW - 1), 0.1),
        "o_w_out": nrm(ks[18], (no, NA_WIDTH, D_MODEL), NA_WIDTH ** -0.5),
        "final_norm_g": 1.0 + nrm(ks[19], (D_MODEL,), 0.02),
    }


def reference(x, meta_tokens, e_norm_g, e_w_in, e_conv_w, e_conv_b, e_dt_bias, e_A_log, e_D,
              e_ssd_norm_g, e_dw_w, e_dw_b, e_ln_g, e_ln_b, e_w_out,
              o_norm_g, o_w_in, o_rpb, o_w_out, final_norm_g):
    b = x.shape[0]
    meta = jnp.broadcast_to(meta_tokens.astype(x.dtype)[None], (b, N_META, x.shape[-1]))
    h = jnp.concatenate([meta, x], axis=1)
    for i in range(DEPTH):
        j = i // 2
        if i % 2 == 0:
            h = even_layer(h, e_norm_g[j], e_w_in[j], e_conv_w[j], e_conv_b[j], e_dt_bias[j],
                           e_A_log[j], e_D[j], e_ssd_norm_g[j], e_dw_w[j], e_dw_b[j],
                           e_ln_g[j], e_ln_b[j], e_w_out[j])
        else:
            h = odd_layer(h, o_norm_g[j], o_w_in[j], o_rpb[j], o_w_out[j])
    h = rmsnorm(h, final_norm_g)
    return h[:, N_META:]
```

```python
import functools

import jax
import jax.numpy as jnp
import numpy as np
from jax import lax
from jax.experimental import pallas as pl
from jax.experimental.pallas import tpu as pltpu

F32 = jnp.float32
BF16 = jnp.bfloat16

D_MODEL = 2048
BATCH = 4
SEQ = 2048
N_META = 16
GRID_W = 64
EPS = 1e-5

SSD_HEAD_DIM = 64
SSD_WIDTH = 2048
SSD_HEADS = 32
SSD_GROUPS = 8
SSD_STATE = 128
SSD_CONV = 5
CHUNK = 128
HEADS_PER_GROUP = SSD_HEADS // SSD_GROUPS
GROUP_W = HEADS_PER_GROUP * SSD_HEAD_DIM

CONV_WIDTH = 2048
CONV_KERNEL = 31
CONV_HALO = CONV_KERNEL // 2

NA_HEAD_DIM = 64
NA_WIDTH = 2048
NA_HEADS = 32
NA_KH = 8
NA_KW = 16
GRID_ROWS = SEQ // GRID_W

PAD_L = (-(N_META + SEQ)) % CHUNK
T_PAD = PAD_L + N_META + SEQ
N_CHUNKS = T_PAD // CHUNK
RAW_OFF = 8
NEG_BIG = -1e30

VMEM_LIMIT_BYTES = 56 * 1024 * 1024

COL_ZA = 0
COL_XS = 2048
COL_B = 4096
COL_C = 5120
COL_ZB = 6144
COL_GV = 8192
COL_GG = 10240
N_MAIN = 12288
DT_COLS = 128


def _cparams(sem):
    return pltpu.CompilerParams(dimension_semantics=sem, vmem_limit_bytes=VMEM_LIMIT_BYTES)


def _dot(a, b):
    return jnp.dot(a, b, preferred_element_type=F32)


def _dot_nt(a, b):
    return lax.dot_general(a, b, (((1,), (1,)), ((), ())), preferred_element_type=F32)


def _silu(x):
    return x * jax.nn.sigmoid(x)


def _softplus(x):
    return jnp.maximum(x, 0.0) + jnp.log1p(jnp.exp(-jnp.abs(x)))


def _split3(v):
    hi = v.astype(BF16)
    r1 = v - hi.astype(F32)
    mid = r1.astype(BF16)
    lo = (r1 - mid.astype(F32)).astype(BF16)
    return hi, mid, lo


def _in_proj_kernel(x_ref, g_ref, w_ref, *rest, with_dt):
    if with_dt:
        wdt_ref, o_ref, dt_ref, u_ref = rest
    else:
        o_ref, u_ref = rest

    @pl.when(pl.program_id(1) == 0)
    def _():
        x = x_ref[...]
        ms = jnp.mean(x * x, axis=-1, keepdims=True)
        y = x * lax.rsqrt(ms + EPS) * g_ref[...]
        u = y.astype(BF16)
        u_ref[...] = u
        if with_dt:
            u_lo = (y - u.astype(F32)).astype(BF16)
            w_hi = wdt_ref[:, :DT_COLS]
            w_lo = wdt_ref[:, DT_COLS:]
            dt_ref[...] = _dot(u, w_hi) + _dot(u_lo, w_hi) + _dot(u, w_lo)

    o_ref[...] = _dot(u_ref[...], w_ref[...]).astype(o_ref.dtype)


def _in_proj(x2d, g, w, wdt=None, *, name):
    m, d = x2d.shape
    n = w.shape[1]
    tm = min(m, 1024)
    tn = 1024
    with_dt = wdt is not None
    in_specs = [
        pl.BlockSpec((tm, d), lambda i, j: (i, 0)),
        pl.BlockSpec((1, d), lambda i, j: (0, 0)),
        pl.BlockSpec((d, tn), lambda i, j: (0, j)),
    ]
    out_shape = [jax.ShapeDtypeStruct((m, n), BF16)]
    out_specs = [pl.BlockSpec((tm, tn), lambda i, j: (i, j))]
    args = [x2d, g.reshape(1, d), w]
    if with_dt:
        in_specs.append(pl.BlockSpec((d, 2 * DT_COLS), lambda i, j: (0, 0)))
        out_shape.append(jax.ShapeDtypeStruct((m, DT_COLS), F32))
        out_specs.append(pl.BlockSpec((tm, DT_COLS), lambda i, j: (i, 0)))
        args.append(wdt)
    res = pl.pallas_call(
        functools.partial(_in_proj_kernel, with_dt=with_dt),
        grid=(m // tm, n // tn),
        in_specs=in_specs,
        out_specs=out_specs,
        out_shape=out_shape,
        scratch_shapes=[pltpu.VMEM((tm, d), BF16)],
        compiler_params=_cparams(("parallel", "arbitrary")),
        name=name,
    )(*args)
    return res if with_dt else res[0]


def _out_proj0_kernel(h_ref, ya_ref, ss_ref, ng_ref, yb_ref, wa_ref, wb_ref, o_ref, yan_ref):
    @pl.when(pl.program_id(1) == 0)
    def _():
        r = lax.rsqrt(ss_ref[:, 0:1] * (1.0 / SSD_WIDTH) + EPS)
        yan_ref[...] = (ya_ref[...].astype(F32) * r * ng_ref[...]).astype(BF16)

    o_ref[...] = h_ref[...] + _dot(yan_ref[...], wa_ref[...]) + _dot(yb_ref[...], wb_ref[...])


def _out_proj0(h2d, ya, ss, ng, yb, wa, wb, *, name):
    m, d = h2d.shape
    tm = min(m, 1024)
    tn = 512
    return pl.pallas_call(
        _out_proj0_kernel,
        grid=(m // tm, d // tn),
        in_specs=[
            pl.BlockSpec((tm, tn), lambda i, j: (i, j)),
            pl.BlockSpec((tm, SSD_WIDTH), lambda i, j: (i, 0)),
            pl.BlockSpec((tm, 128), lambda i, j: (i, 0)),
            pl.BlockSpec((1, SSD_WIDTH), lambda i, j: (0, 0)),
            pl.BlockSpec((tm, CONV_WIDTH), lambda i, j: (i, 0)),
            pl.BlockSpec((SSD_WIDTH, tn), lambda i, j: (0, j)),
            pl.BlockSpec((CONV_WIDTH, tn), lambda i, j: (0, j)),
        ],
        out_specs=pl.BlockSpec((tm, tn), lambda i, j: (i, j)),
        out_shape=jax.ShapeDtypeStruct((m, d), F32),
        scratch_shapes=[pltpu.VMEM((tm, SSD_WIDTH), BF16)],
        compiler_params=_cparams(("parallel", "arbitrary")),
        name=name,
    )(h2d, ya, ss, ng.reshape(1, SSD_WIDTH), yb, wa, wb)


def _out_proj1_kernel(h_ref, o_ref, w_ref, g_ref, out_ref):
    hn = h_ref[...] + _dot(o_ref[...], w_ref[...])
    ms = jnp.mean(hn * hn, axis=-1, keepdims=True)
    out_ref[...] = hn * lax.rsqrt(ms + EPS) * g_ref[...]


def _out_proj1(h2d, o, w, g, *, name):
    m, d = h2d.shape
    tm = 512
    return pl.pallas_call(
        _out_proj1_kernel,
        grid=(m // tm,),
        in_specs=[
            pl.BlockSpec((tm, d), lambda i: (i, 0)),
            pl.BlockSpec((tm, NA_WIDTH), lambda i: (i, 0)),
            pl.BlockSpec((NA_WIDTH, d), lambda i: (0, 0)),
            pl.BlockSpec((1, d), lambda i: (0, 0)),
        ],
        out_specs=pl.BlockSpec((tm, d), lambda i: (i, 0)),
        out_shape=jax.ShapeDtypeStruct((m, d), F32),
        compiler_params=_cparams(("parallel",)),
        name=name,
    )(h2d, o, w, g.reshape(1, d))


def _ssd_kernel(za_x, xs_x, bm_x, cm_x, za_m, xs_m, bm_m, cm_m, dtc_ref, dtt_ref,
                cwx, cwb, cwc, cbx, cbb, cbc, biasc, biast, alogc, alogt, dsk,
                yg_x, yg_m, ss_x, ss_m,
                raw_x, raw_b, raw_c, xc, bc, cc, ybuf, acol, dtvt, at, st):
    g = pl.program_id(1)

    for raw, m_ref, x_ref in ((raw_x, xs_m, xs_x), (raw_b, bm_m, bm_x), (raw_c, cm_m, cm_x)):
        w = raw.shape[1]
        raw[0:RAW_OFF + PAD_L, :] = jnp.zeros((RAW_OFF + PAD_L, w), F32)
        raw[RAW_OFF + PAD_L:RAW_OFF + CHUNK, :] = m_ref[...].astype(F32)
        raw[RAW_OFF + CHUNK:RAW_OFF + T_PAD, :] = x_ref[...].astype(F32)
        raw[RAW_OFF + T_PAD:RAW_OFF + T_PAD + 8, :] = jnp.zeros((8, w), F32)

    row = lax.broadcasted_iota(jnp.int32, (CHUNK, 1), 0)

    def conv_chunk(c, carry):
        base = pl.multiple_of(c * CHUNK, CHUNK)
        keep = jnp.logical_or(c > 0, row >= PAD_L)
        for raw, cw, cb, dst in ((raw_x, cwx, cbx, xc), (raw_b, cwb, cbb, bc), (raw_c, cwc, cbc, cc)):
            win = raw[pl.ds(base, CHUNK + 16), :]
            acc = cb[...]
            for k in range(SSD_CONV):
                s0 = RAW_OFF - SSD_CONV // 2 + k
                acc = acc + win[s0:s0 + CHUNK, :] * cw[k:k + 1, :]
            v = jnp.where(keep, _silu(acc), 0.0)
            dst[pl.ds(base, CHUNK), :] = v.astype(dst.dtype)
        return carry

    lax.fori_loop(0, N_CHUNKS, conv_chunk, 0)

    a_c = -jnp.exp(alogc[0])
    acol[...] = _softplus(dtc_ref[0, 0] + biasc[0]) * a_c
    a_t = -jnp.exp(alogt[0])
    dtv = _softplus(dtt_ref[0, 0] + biast[0])
    dtvt[...] = dtv
    at[...] = dtv * a_t

    li = lax.broadcasted_iota(jnp.int32, (CHUNK, CHUNK), 0)
    si = lax.broadcasted_iota(jnp.int32, (CHUNK, CHUNK), 1)
    lane = lax.broadcasted_iota(jnp.int32, (1, CHUNK), 1)
    lane_lo = lane < SSD_HEAD_DIM
    head_sel = (jnp.where(lane_lo, 1.0, 0.0).astype(BF16), jnp.where(lane_lo, 0.0, 1.0).astype(BF16))

    def run_direction(d):
        if d == 0:
            mask = li >= si
            tri_row = li <= si
            end = CHUNK - 1
        else:
            mask = li <= si
            tri_row = li >= si
            end = 0
        t_col = jnp.where(mask, 1.0, 0.0).astype(BF16)
        t_row = jnp.where(tri_row, 1.0, 0.0).astype(BF16)
        st[...] = jnp.zeros_like(st)

        def body(i, carry):
            c = i if d == 0 else N_CHUNKS - 1 - i
            base = pl.multiple_of(c * CHUNK, CHUNK)
            xb = xc[pl.ds(base, CHUNK), :].astype(BF16)
            bb = bc[pl.ds(base, CHUNK), :]
            cb_ = cc[pl.ds(base, CHUNK), :]
            a_row = at[:, pl.ds(base, CHUNK)]
            dt_row = dtvt[:, pl.ds(base, CHUNK)]
            a_col = acol[pl.ds(base, CHUNK), :]
            h1, h2, h3 = _split3(a_row)
            cs_row = _dot(h1, t_row) + _dot(h2, t_row) + _dot(h3, t_row)
            v1, v2, v3 = _split3(a_col)
            cs_col = _dot(t_col, v1) + _dot(t_col, v2) + _dot(t_col, v3)
            cbm = _dot_nt(cb_, bb)
            bt = bb.astype(F32).T
            cf = cb_.astype(F32)
            for pair in range(2):
                ps = slice(pair * CHUNK, (pair + 1) * CHUNK)
                xp = xb[:, ps]
                sp = st[:, ps]
                spb = sp.astype(BF16)
                y = jnp.zeros((CHUNK, CHUNK), F32)
                new_s = jnp.zeros((CHUNK, CHUNK), F32)
                decs = []
                for e in range(2):
                    k = 2 * pair + e
                    xk = xp * head_sel[e]
                    sk = spb * head_sel[e]
                    hd = 4 * d + k
                    csb = jnp.broadcast_to(cs_col[:, hd:hd + 1], (CHUNK, CHUNK))
                    csr = cs_row[hd:hd + 1, :]
                    dtr = dt_row[hd:hd + 1, :]
                    lmat = jnp.exp(jnp.where(mask, csb - csr, NEG_BIG))
                    mh = (cbm * lmat * dtr).astype(BF16)
                    csc = (cf * jnp.exp(csb)).astype(BF16)
                    y = y + _dot(mh, xk) + _dot(csc, sk)
                    wend = cs_row[hd:hd + 1, end:end + 1]
                    wrow = jnp.exp(wend - csr) * dtr
                    new_s = new_s + _dot((bt * wrow).astype(BF16), xk)
                    decs.append(jnp.exp(wend))
                dec = jnp.where(lane_lo, decs[0], decs[1])
                st[:, ps] = sp * dec + new_s
                if d == 0:
                    ybuf[pl.ds(base, CHUNK), ps] = y
                else:
                    ybuf[pl.ds(base, CHUNK), ps] = ybuf[pl.ds(base, CHUNK), ps] + y
            return carry

        lax.fori_loop(0, N_CHUNKS, body, 0)

    run_direction(0)
    run_direction(1)

    @pl.when(g == 0)
    def _():
        ss_x[...] = jnp.zeros_like(ss_x)
        ss_m[...] = jnp.zeros_like(ss_m)

    def finish(i, carry):
        base = pl.multiple_of(i * CHUNK, CHUNK)
        y = ybuf[pl.ds(CHUNK + base, CHUNK), :] + dsk[...] * xc[pl.ds(CHUNK + base, CHUNK), :]
        yg = y * _silu(za_x[pl.ds(base, CHUNK), :].astype(F32))
        yg_x[pl.ds(base, CHUNK), :] = yg.astype(yg_x.dtype)
        ss_x[pl.ds(base, CHUNK), :] = ss_x[pl.ds(base, CHUNK), :] + jnp.sum(yg * yg, axis=-1, keepdims=True)
        return carry

    lax.fori_loop(0, SEQ // CHUNK, finish, 0)

    y = ybuf[PAD_L:CHUNK, :] + dsk[...] * xc[PAD_L:CHUNK, :]
    yg = y * _silu(za_m[...].astype(F32))
    yg_m[...] = yg.astype(yg_m.dtype)
    ss_m[...] = ss_m[...] + jnp.sum(yg * yg, axis=-1, keepdims=True)


def _ssd(proj_x, proj_m, dtc, dtt, conv_w, conv_b, biasc, biast, alogc, alogt, dsk):
    gw, sn = GROUP_W, SSD_STATE
    nb = BATCH

    def xspec(width, col0):
        return pl.BlockSpec((SEQ, width), lambda b, g, c=col0 // width: (b, c + g))

    def mspec(width, col0):
        return pl.BlockSpec((N_META, width), lambda b, g, c=col0 // width: (0, c + g))

    def wspec(rows, width, col0):
        return pl.BlockSpec((rows, width), lambda b, g, c=col0 // width: (0, c + g))

    in_specs = [
        xspec(gw, COL_ZA), xspec(gw, COL_XS), xspec(sn, COL_B), xspec(sn, COL_C),
        mspec(gw, COL_ZA), mspec(gw, COL_XS), mspec(sn, COL_B), mspec(sn, COL_C),
        pl.BlockSpec((1, 1, T_PAD, 8), lambda b, g: (b, g, 0, 0)),
        pl.BlockSpec((1, 1, 8, T_PAD), lambda b, g: (b, g, 0, 0)),
        wspec(SSD_CONV, gw, 0), wspec(SSD_CONV, sn, SSD_WIDTH), wspec(SSD_CONV, sn, SSD_WIDTH + SSD_GROUPS * sn),
        wspec(1, gw, 0), wspec(1, sn, SSD_WIDTH), wspec(1, sn, SSD_WIDTH + SSD_GROUPS * sn),
        pl.BlockSpec((1, 1, 8), lambda b, g: (g, 0, 0)),
        pl.BlockSpec((1, 8, 1), lambda b, g: (g, 0, 0)),
        pl.BlockSpec((1, 1, 8), lambda b, g: (g, 0, 0)),
        pl.BlockSpec((1, 8, 1), lambda b, g: (g, 0, 0)),
        pl.BlockSpec((1, gw), lambda b, g: (0, g)),
    ]
    out_shape = [
        jax.ShapeDtypeStruct((nb * SEQ, SSD_WIDTH), BF16),
        jax.ShapeDtypeStruct((nb * N_META, SSD_WIDTH), BF16),
        jax.ShapeDtypeStruct((nb * SEQ, 128), F32),
        jax.ShapeDtypeStruct((nb * N_META, 128), F32),
    ]
    out_specs = [
        pl.BlockSpec((SEQ, gw), lambda b, g: (b, g)),
        pl.BlockSpec((N_META, gw), lambda b, g: (b, g)),
        pl.BlockSpec((SEQ, 128), lambda b, g: (b, 0)),
        pl.BlockSpec((N_META, 128), lambda b, g: (b, 0)),
    ]
    raw_rows = RAW_OFF + T_PAD + 8
    scratch = [
        pltpu.VMEM((raw_rows, gw), F32), pltpu.VMEM((raw_rows, sn), F32), pltpu.VMEM((raw_rows, sn), F32),
        pltpu.VMEM((T_PAD, gw), F32), pltpu.VMEM((T_PAD, sn), BF16), pltpu.VMEM((T_PAD, sn), BF16),
        pltpu.VMEM((T_PAD, gw), F32),
        pltpu.VMEM((T_PAD, 8), F32), pltpu.VMEM((8, T_PAD), F32), pltpu.VMEM((8, T_PAD), F32),
        pltpu.VMEM((sn, gw), F32),
    ]
    return pl.pallas_call(
        _ssd_kernel,
        grid=(nb, SSD_GROUPS),
        in_specs=in_specs,
        out_specs=out_specs,
        out_shape=out_shape,
        scratch_shapes=scratch,
        compiler_params=_cparams(("parallel", "arbitrary")),
        name="ssd_mixer",
    )(proj_x, proj_x, proj_x, proj_x, proj_m, proj_m, proj_m, proj_m, dtc, dtt,
      conv_w, conv_w, conv_w, conv_b, conv_b, conv_b, biasc, biast, alogc, alogt, dsk)


CONV_TT = 512
CONV_RC = 32
CONV_CC = 512


def _conv_kernel(gv, gg, zb, gv_p, gg_p, gv_n, gg_n, gv_m, gg_m, zb_m, dww, dwb, lng, lnb,
                 yb_x, yb_m, ubuf, cbuf):
    i = pl.program_id(1)
    nt = pl.num_programs(1)
    tt = CONV_TT

    def glu(v_ref, g_ref):
        return v_ref[...].astype(F32) * jax.nn.sigmoid(g_ref[...].astype(F32))

    ubuf[0:16, :] = jnp.zeros((16, CONV_WIDTH), F32)
    ubuf[32:32 + tt, :] = glu(gv, gg)

    @pl.when(i == 0)
    def _():
        ubuf[16:32, :] = glu(gv_m, gg_m)

    @pl.when(i > 0)
    def _():
        ubuf[16:32, :] = glu(gv_p, gg_p)

    @pl.when(i == nt - 1)
    def _():
        ubuf[32 + tt:48 + tt, :] = jnp.zeros((16, CONV_WIDTH), F32)

    @pl.when(i < nt - 1)
    def _():
        ubuf[32 + tt:48 + tt, :] = glu(gv_n, gg_n)

    def conv_rows(win_base, nrows, cs):
        win = ubuf[pl.ds(win_base, nrows + 32), cs]
        acc = jnp.broadcast_to(dwb[:, cs], (nrows, cs.stop - cs.start))
        for k in range(CONV_KERNEL):
            acc = acc + win[k + 1:k + 1 + nrows, :] * dww[k:k + 1, cs]
        return acc

    def norm_gate(c_rows, z_rows):
        mu = jnp.mean(c_rows, axis=-1, keepdims=True)
        var = jnp.mean(jnp.square(c_rows - mu), axis=-1, keepdims=True)
        yn = (c_rows - mu) * lax.rsqrt(var + EPS) * lng[...] + lnb[...]
        return _silu(yn) * _silu(z_rows)

    def chunk(rc, carry):
        base = pl.multiple_of(rc * CONV_RC, CONV_RC)
        for cc in range(CONV_WIDTH // CONV_CC):
            cs = slice(cc * CONV_CC, (cc + 1) * CONV_CC)
            cbuf[pl.ds(base, CONV_RC), cs] = conv_rows(base + 16, CONV_RC, cs)
        out = norm_gate(cbuf[pl.ds(base, CONV_RC), :], zb[pl.ds(base, CONV_RC), :].astype(F32))
        yb_x[pl.ds(base, CONV_RC), :] = out.astype(yb_x.dtype)
        return carry

    lax.fori_loop(0, tt // CONV_RC, chunk, 0)

    @pl.when(i == 0)
    def _():
        for cc in range(CONV_WIDTH // CONV_CC):
            cs = slice(cc * CONV_CC, (cc + 1) * CONV_CC)
            cbuf[0:16, cs] = conv_rows(0, 16, cs)
        out = norm_gate(cbuf[0:16, :], zb_m[...].astype(F32))
        yb_m[...] = out.astype(yb_m.dtype)


def _conformer_conv(proj_x, proj_m, dw_w, dw_b, ln_g, ln_b):
    tt = CONV_TT
    nt = SEQ // tt
    w = CONV_WIDTH
    hb = tt // 16
    last_hb = BATCH * SEQ // 16 - 1

    def main(col0):
        return pl.BlockSpec((tt, w), lambda b, i, c=col0 // w: (b * nt + i, c))

    def prev(col0):
        return pl.BlockSpec((16, w), lambda b, i, c=col0 // w: (jnp.maximum((b * nt + i) * hb - 1, 0), c))

    def nxt(col0):
        return pl.BlockSpec((16, w), lambda b, i, c=col0 // w: (jnp.minimum((b * nt + i + 1) * hb, last_hb), c))

    def meta(col0):
        return pl.BlockSpec((N_META, w), lambda b, i, c=col0 // w: (0, c))

    vec = pl.BlockSpec((1, w), lambda b, i: (0, 0))
    return pl.pallas_call(
        _conv_kernel,
        grid=(BATCH, nt),
        in_specs=[
            main(COL_GV), main(COL_GG), main(COL_ZB),
            prev(COL_GV), prev(COL_GG), nxt(COL_GV), nxt(COL_GG),
            meta(COL_GV), meta(COL_GG), meta(COL_ZB),
            pl.BlockSpec((CONV_KERNEL, w), lambda b, i: (0, 0)), vec, vec, vec,
        ],
        out_specs=[
            pl.BlockSpec((tt, w), lambda b, i: (b * nt + i, 0)),
            pl.BlockSpec((N_META, w), lambda b, i: (b, 0)),
        ],
        out_shape=[
            jax.ShapeDtypeStruct((BATCH * SEQ, w), BF16),
            jax.ShapeDtypeStruct((BATCH * N_META, w), BF16),
        ],
        scratch_shapes=[pltpu.VMEM((tt + 48, w), F32), pltpu.VMEM((tt, w), F32)],
        compiler_params=_cparams(("parallel", "arbitrary")),
        name="conformer_conv",
    )(proj_x, proj_x, proj_x, proj_x, proj_x, proj_x, proj_x, proj_m, proj_m, proj_m,
      dw_w, dw_b.reshape(1, w), ln_g.reshape(1, w), ln_b.reshape(1, w))


NA_WIN = NA_KH * GRID_W


def _natten_kernel(q_ref, k_ref, v_ref, z_ref, km_ref, vm_ref, bias_ref, o_ref):
    lane = lax.broadcasted_iota(jnp.int32, (1, 2 * NA_HEAD_DIM), 1)
    lane_lo = lane < NA_HEAD_DIM
    scale = NA_HEAD_DIM ** -0.5
    q_sel = (jnp.where(lane_lo, scale, 0.0).astype(BF16), jnp.where(lane_lo, 0.0, scale).astype(BF16))
    km = km_ref[...]
    vm = vm_ref[...]

    def row_block(r, carry):
        rs = jnp.clip(r - NA_KH // 2, 0, GRID_ROWS - NA_KH)
        cls = rs - r + NA_KH - 1
        qs = pl.multiple_of(r * GRID_W, GRID_W)
        ks = pl.multiple_of(rs * GRID_W, GRID_W)
        q = q_ref[pl.ds(qs, GRID_W), :]
        kw = k_ref[pl.ds(ks, NA_WIN), :]
        vw = v_ref[pl.ds(ks, NA_WIN), :]
        outs = []
        for e in range(2):
            qe = q * q_sel[e]
            s = _dot_nt(qe, kw) + bias_ref[e, cls]
            sm = _dot_nt(qe, km)
            m = jnp.maximum(jnp.max(s, axis=-1, keepdims=True), jnp.max(sm, axis=-1, keepdims=True))
            p = jnp.exp(s - m)
            pm = jnp.exp(sm - m)
            den = jnp.sum(p, axis=-1, keepdims=True) + jnp.sum(pm, axis=-1, keepdims=True)
            outs.append((_dot(p.astype(BF16), vw) + _dot(pm.astype(BF16), vm)) / den)
        o = jnp.where(lane_lo, outs[0], outs[1])
        o = o * _silu(z_ref[pl.ds(qs, GRID_W), :].astype(F32))
        o_ref[pl.ds(qs, GRID_W), :] = o.astype(o_ref.dtype)
        return carry

    lax.fori_loop(0, GRID_ROWS, row_block, 0)


def _natten(proj_x, proj_m, bias_tbl):
    pw = 2 * NA_HEAD_DIM
    npair = NA_HEADS // 2

    def xspec(col0):
        return pl.BlockSpec((SEQ, pw), lambda b, h, c=col0 // pw: (b, c + h))

    def mspec(col0):
        return pl.BlockSpec((N_META, pw), lambda b, h, c=col0 // pw: (b, c + h))

    return pl.pallas_call(
        _natten_kernel,
        grid=(BATCH, npair),
        in_specs=[
            xspec(0), xspec(NA_WIDTH), xspec(2 * NA_WIDTH), xspec(3 * NA_WIDTH),
            mspec(NA_WIDTH), mspec(2 * NA_WIDTH),
            pl.BlockSpec((2, NA_KH, GRID_W, NA_WIN), lambda b, h: (h, 0, 0, 0)),
        ],
        out_specs=pl.BlockSpec((SEQ, pw), lambda b, h: (b, h)),
        out_shape=jax.ShapeDtypeStruct((BATCH * SEQ, NA_WIDTH), BF16),
        compiler_params=_cparams(("parallel", "arbitrary")),
        name="natten",
    )(proj_x, proj_x, proj_x, proj_x, proj_m, proj_m, bias_tbl)


def _natten_bias_table(rpb):
    col = np.arange(GRID_W)
    col_start = np.clip(col - NA_KW // 2, 0, GRID_W - NA_KW)
    col_mask = (col[None, :] >= col_start[:, None]) & (col[None, :] < col_start[:, None] + NA_KW)
    col_idx = np.clip(col[None, :] - col[:, None] + NA_KW - 1, 0, 2 * NA_KW - 2)
    bias_cols = jnp.where(col_mask[None, None], rpb[:, :, col_idx].astype(F32), NEG_BIG)
    dr = np.arange(NA_KH)[:, None] + np.arange(NA_KH)[None, :]
    tbl = bias_cols[:, dr]
    tbl = jnp.transpose(tbl, (0, 1, 3, 2, 4))
    return tbl.reshape(NA_HEADS, NA_KH, GRID_W, NA_WIN)


def kernel(x, meta_tokens, e_norm_g, e_w_in, e_conv_w, e_conv_b, e_dt_bias, e_A_log, e_D, e_ssd_norm_g, e_dw_w, e_dw_b, e_ln_g, e_ln_b, e_w_out, o_norm_g, o_w_in, o_rpb, o_w_out, final_norm_g):
    nb = x.shape[0]
    x2d = x.reshape(nb * SEQ, D_MODEL)
    meta = meta_tokens.astype(x.dtype)

    w_in = e_w_in[0]
    dt0 = COL_ZB
    w_main = jnp.concatenate([w_in[:, :dt0], w_in[:, dt0 + 2 * SSD_HEADS:]], axis=1).astype(BF16)
    w_dt = jnp.pad(w_in[:, dt0:dt0 + 2 * SSD_HEADS], ((0, 0), (0, DT_COLS - 2 * SSD_HEADS)))
    w_dt_hi = w_dt.astype(BF16)
    w_dt_lo = (w_dt - w_dt_hi.astype(F32)).astype(BF16)
    wdt = jnp.concatenate([w_dt_hi, w_dt_lo], axis=1)

    proj_x, dt_x = _in_proj(x2d, e_norm_g[0], w_main, wdt, name="in_proj0")
    proj_m, dt_m = _in_proj(meta, e_norm_g[0], w_main, wdt, name="in_proj0_meta")

    def group_dt(a, rows):
        a = a[:, :2 * SSD_HEADS].reshape(-1, rows, 2, SSD_GROUPS, HEADS_PER_GROUP)
        return jnp.transpose(a, (0, 3, 1, 2, 4)).reshape(-1, SSD_GROUPS, rows, 2 * HEADS_PER_GROUP)

    dt_seq = jnp.concatenate([
        jnp.full((nb, SSD_GROUPS, PAD_L, 2 * HEADS_PER_GROUP), NEG_BIG, F32),
        jnp.broadcast_to(group_dt(dt_m, N_META), (nb, SSD_GROUPS, N_META, 2 * HEADS_PER_GROUP)),
        group_dt(dt_x, SEQ),
    ], axis=2)
    dt_seq_t = jnp.swapaxes(dt_seq, 2, 3)

    def group_vec(a):
        return jnp.transpose(a.reshape(2, SSD_GROUPS, HEADS_PER_GROUP), (1, 0, 2)).reshape(SSD_GROUPS, -1)

    bias_g = group_vec(e_dt_bias[0].astype(F32))
    alog_g = group_vec(e_A_log[0].astype(F32))
    dsk = jnp.repeat(e_D[0].astype(F32), SSD_HEAD_DIM).reshape(1, SSD_WIDTH)

    yg_x, yg_m, ss_x, ss_m = _ssd(
        proj_x, proj_m, dt_seq, dt_seq_t, e_conv_w[0], e_conv_b[0].reshape(1, -1),
        bias_g[:, None, :], bias_g[:, :, None], alog_g[:, None, :], alog_g[:, :, None], dsk)
    yb_x, yb_m = _conformer_conv(proj_x, proj_m, e_dw_w[0], e_dw_b[0], e_ln_g[0], e_ln_b[0])

    w_out = e_w_out[0].astype(BF16)
    wa, wb = w_out[:SSD_WIDTH], w_out[SSD_WIDTH:]
    h1_x = _out_proj0(x2d, yg_x, ss_x, e_ssd_norm_g[0], yb_x, wa, wb, name="out_proj0")
    meta_b = jnp.broadcast_to(meta[None], (nb, N_META, D_MODEL)).reshape(nb * N_META, D_MODEL)
    h1_m = _out_proj0(meta_b, yg_m, ss_m, e_ssd_norm_g[0], yb_m, wa, wb, name="out_proj0_meta")

    w_in1 = o_w_in[0].astype(BF16)
    p1_x = _in_proj(h1_x, o_norm_g[0], w_in1, name="in_proj1")
    p1_m = _in_proj(h1_m, o_norm_g[0], w_in1, name="in_proj1_meta")
    o_x = _natten(p1_x, p1_m, _natten_bias_table(o_rpb[0]))
    out = _out_proj1(h1_x, o_x, o_w_out[0].astype(BF16), final_norm_g, name="out_proj1")
    return out.reshape(nb, SEQ, D_MODEL)
```

```python
import functools

import jax
import jax.numpy as jnp
import numpy as np
from jax import lax
from jax.experimental import pallas as pl
from jax.experimental.pallas import tpu as pltpu

F32 = jnp.float32
BF16 = jnp.bfloat16

D_MODEL = 2048
BATCH = 4
SEQ = 2048
N_META = 16
GRID_W = 64
EPS = 1e-5

SSD_HEAD_DIM = 64
SSD_WIDTH = 2048
SSD_HEADS = 32
SSD_GROUPS = 8
SSD_STATE = 128
SSD_CONV = 5
CHUNK = 128
HEADS_PER_GROUP = SSD_HEADS // SSD_GROUPS
GROUP_W = HEADS_PER_GROUP * SSD_HEAD_DIM

CONV_WIDTH = 2048
CONV_KERNEL = 31
CONV_HALO = CONV_KERNEL // 2

NA_HEAD_DIM = 64
NA_WIDTH = 2048
NA_HEADS = 32
NA_KH = 8
NA_KW = 16
GRID_ROWS = SEQ // GRID_W

PAD_L = (-(N_META + SEQ)) % CHUNK
T_PAD = PAD_L + N_META + SEQ
N_CHUNKS = T_PAD // CHUNK
RAW_OFF = 8
NEG_BIG = -1e30

VMEM_LIMIT_BYTES = 56 * 1024 * 1024

COL_ZA = 0
COL_XS = 2048
COL_B = 4096
COL_C = 5120
COL_ZB = 6144
COL_GV = 8192
COL_GG = 10240
N_MAIN = 12288
DT_COLS = 128


def _cparams(sem):
    return pltpu.CompilerParams(dimension_semantics=sem, vmem_limit_bytes=VMEM_LIMIT_BYTES)


def _dot(a, b):
    return jnp.dot(a, b, preferred_element_type=F32)


def _dot_nt(a, b):
    return lax.dot_general(a, b, (((1,), (1,)), ((), ())), preferred_element_type=F32)


def _silu(x):
    return x * jax.nn.sigmoid(x)


def _softplus(x):
    return jnp.maximum(x, 0.0) + jnp.log1p(jnp.exp(-jnp.abs(x)))


def _split3(v):
    hi = v.astype(BF16)
    r1 = v - hi.astype(F32)
    mid = r1.astype(BF16)
    lo = (r1 - mid.astype(F32)).astype(BF16)
    return hi, mid, lo


def _in_proj_kernel(x_ref, g_ref, w_ref, *rest, with_dt, w_is_nk):
    if with_dt:
        wdt_ref, o_ref, dt_ref, u_ref = rest
    else:
        o_ref, u_ref = rest

    @pl.when(pl.program_id(1) == 0)
    def _():
        x = x_ref[...]
        ms = jnp.mean(x * x, axis=-1, keepdims=True)
        y = x * lax.rsqrt(ms + EPS) * g_ref[...]
        u = y.astype(BF16)
        u_ref[...] = u
        if with_dt:
            u_lo = (y - u.astype(F32)).astype(BF16)
            w_hi = wdt_ref[:, :DT_COLS]
            w_lo = wdt_ref[:, DT_COLS:]
            dt_ref[...] = _dot(u, w_hi) + _dot(u_lo, w_hi) + _dot(u, w_lo)

    mm = _dot_nt if w_is_nk else _dot
    o_ref[...] = mm(u_ref[...], w_ref[...]).astype(o_ref.dtype)


def _in_proj(x2d, g, w, wdt=None, *, w_is_nk=False, name):
    m, d = x2d.shape
    n = w.shape[0] if w_is_nk else w.shape[1]
    tm = min(m, 1024)
    tn = 1024
    with_dt = wdt is not None
    in_specs = [
        pl.BlockSpec((tm, d), lambda i, j: (i, 0)),
        pl.BlockSpec((1, d), lambda i, j: (0, 0)),
        pl.BlockSpec((tn, d), lambda i, j: (j, 0)) if w_is_nk else pl.BlockSpec((d, tn), lambda i, j: (0, j)),
    ]
    out_shape = [jax.ShapeDtypeStruct((m, n), BF16)]
    out_specs = [pl.BlockSpec((tm, tn), lambda i, j: (i, j))]
    args = [x2d, g.reshape(1, d), w]
    if with_dt:
        in_specs.append(pl.BlockSpec((d, 2 * DT_COLS), lambda i, j: (0, 0)))
        out_shape.append(jax.ShapeDtypeStruct((m, DT_COLS), F32))
        out_specs.append(pl.BlockSpec((tm, DT_COLS), lambda i, j: (i, 0)))
        args.append(wdt)
    res = pl.pallas_call(
        functools.partial(_in_proj_kernel, with_dt=with_dt, w_is_nk=w_is_nk),
        grid=(m // tm, n // tn),
        in_specs=in_specs,
        out_specs=out_specs,
        out_shape=out_shape,
        scratch_shapes=[pltpu.VMEM((tm, d), BF16)],
        compiler_params=_cparams(("parallel", "arbitrary")),
        name=name,
    )(*args)
    return res if with_dt else res[0]


def _out_proj0_kernel(h_ref, ya_ref, ss_ref, ng_ref, yb_ref, wa_ref, wb_ref, o_ref, yan_ref):
    @pl.when(pl.program_id(1) == 0)
    def _():
        r = lax.rsqrt(ss_ref[:, 0:1] * (1.0 / SSD_WIDTH) + EPS)
        yan_ref[...] = (ya_ref[...].astype(F32) * r * ng_ref[...]).astype(BF16)

    o_ref[...] = h_ref[...] + _dot(yan_ref[...], wa_ref[...]) + _dot(yb_ref[...], wb_ref[...])


def _out_proj0(h2d, ya, ss, ng, yb, wa, wb, *, name):
    m, d = h2d.shape
    tm = min(m, 1024)
    tn = 512
    return pl.pallas_call(
        _out_proj0_kernel,
        grid=(m // tm, d // tn),
        in_specs=[
            pl.BlockSpec((tm, tn), lambda i, j: (i, j)),
            pl.BlockSpec((tm, SSD_WIDTH), lambda i, j: (i, 0)),
            pl.BlockSpec((tm, 128), lambda i, j: (i, 0)),
            pl.BlockSpec((1, SSD_WIDTH), lambda i, j: (0, 0)),
            pl.BlockSpec((tm, CONV_WIDTH), lambda i, j: (i, 0)),
            pl.BlockSpec((SSD_WIDTH, tn), lambda i, j: (0, j)),
            pl.BlockSpec((CONV_WIDTH, tn), lambda i, j: (0, j)),
        ],
        out_specs=pl.BlockSpec((tm, tn), lambda i, j: (i, j)),
        out_shape=jax.ShapeDtypeStruct((m, d), F32),
        scratch_shapes=[pltpu.VMEM((tm, SSD_WIDTH), BF16)],
        compiler_params=_cparams(("parallel", "arbitrary")),
        name=name,
    )(h2d, ya, ss, ng.reshape(1, SSD_WIDTH), yb, wa, wb)


def _out_proj1_kernel(h_ref, o_ref, w_ref, g_ref, out_ref):
    hn = h_ref[...] + _dot(o_ref[...], w_ref[...])
    ms = jnp.mean(hn * hn, axis=-1, keepdims=True)
    out_ref[...] = hn * lax.rsqrt(ms + EPS) * g_ref[...]


def _out_proj1(h2d, o, w, g, *, name):
    m, d = h2d.shape
    tm = 512
    return pl.pallas_call(
        _out_proj1_kernel,
        grid=(m // tm,),
        in_specs=[
            pl.BlockSpec((tm, d), lambda i: (i, 0)),
            pl.BlockSpec((tm, NA_WIDTH), lambda i: (i, 0)),
            pl.BlockSpec((NA_WIDTH, d), lambda i: (0, 0)),
            pl.BlockSpec((1, d), lambda i: (0, 0)),
        ],
        out_specs=pl.BlockSpec((tm, d), lambda i: (i, 0)),
        out_shape=jax.ShapeDtypeStruct((m, d), F32),
        compiler_params=_cparams(("parallel",)),
        name=name,
    )(h2d, o, w, g.reshape(1, d))


def _ssd_kernel(za_x, xs_x, bm_x, cm_x, za_m, xs_m, bm_m, cm_m, dtt_ref,
                cwx, cwb, cwc, cbx, cbb, cbc, biast, alogt, dsk,
                yg_x, yg_m, ss_x, ss_m,
                raw_x, raw_b, raw_c, xc, bc, cc, ybuf_f, ybuf_b, dtvt, at, cs_t, st_f, st_b):
    g = pl.program_id(1)

    for raw, m_ref, x_ref in ((raw_x, xs_m, xs_x), (raw_b, bm_m, bm_x), (raw_c, cm_m, cm_x)):
        w = raw.shape[1]
        raw[0:RAW_OFF + PAD_L, :] = jnp.zeros((RAW_OFF + PAD_L, w), F32)
        raw[RAW_OFF + PAD_L:RAW_OFF + CHUNK, :] = m_ref[...].astype(F32)
        raw[RAW_OFF + CHUNK:RAW_OFF + T_PAD, :] = x_ref[...].astype(F32)
        raw[RAW_OFF + T_PAD:RAW_OFF + T_PAD + 8, :] = jnp.zeros((8, w), F32)

    row = lax.broadcasted_iota(jnp.int32, (CHUNK, 1), 0)

    def conv_chunk(c, carry):
        base = pl.multiple_of(c * CHUNK, CHUNK)
        keep = jnp.logical_or(c > 0, row >= PAD_L)
        for raw, cw, cb, dst in ((raw_x, cwx, cbx, xc), (raw_b, cwb, cbb, bc), (raw_c, cwc, cbc, cc)):
            win = raw[pl.ds(base, CHUNK + 16), :]
            acc = cb[...]
            for k in range(SSD_CONV):
                s0 = RAW_OFF - SSD_CONV // 2 + k
                acc = acc + win[s0:s0 + CHUNK, :] * cw[k:k + 1, :]
            v = jnp.where(keep, _silu(acc), 0.0)
            dst[pl.ds(base, CHUNK), :] = v.astype(dst.dtype)
        return carry

    lax.fori_loop(0, N_CHUNKS, conv_chunk, 0)

    a_t = -jnp.exp(alogt[0])
    dtv = _softplus(dtt_ref[0, 0] + biast[0])
    dtvt[...] = dtv
    at[...] = dtv * a_t

    li = lax.broadcasted_iota(jnp.int32, (CHUNK, CHUNK), 0)
    si = lax.broadcasted_iota(jnp.int32, (CHUNK, CHUNK), 1)
    lane = lax.broadcasted_iota(jnp.int32, (1, CHUNK), 1)
    lane_lo = lane < SSD_HEAD_DIM
    head_sel = (jnp.where(lane_lo, 1.0, 0.0).astype(BF16), jnp.where(lane_lo, 0.0, 1.0).astype(BF16))
    masks = (li >= si, li <= si)
    ends = (CHUNK - 1, 0)

    a2 = jnp.concatenate([at[:, c * CHUNK:(c + 1) * CHUNK] for c in range(N_CHUNKS)]
                         + [jnp.zeros((8, CHUNK), F32)], axis=0)
    pieces = _split3(a2)
    t_f = jnp.where(li <= si, 1.0, 0.0).astype(BF16)
    t_b = jnp.where(li >= si, 1.0, 0.0).astype(BF16)
    cs_f = sum(_dot(p, t_f) for p in pieces)
    cs_b = sum(_dot(p, t_b) for p in pieces)
    is_fwd = lax.rem(lax.broadcasted_iota(jnp.int32, (8 * (N_CHUNKS + 1), 1), 0), 8) < HEADS_PER_GROUP
    cs2 = jnp.where(is_fwd, cs_f, cs_b)
    for c in range(N_CHUNKS):
        cs_t[:, c * CHUNK:(c + 1) * CHUNK] = cs2[8 * c:8 * c + 8]

    st_f[...] = jnp.zeros_like(st_f)
    st_b[...] = jnp.zeros_like(st_b)

    def chunk_inputs(c):
        base = pl.multiple_of(c * CHUNK, CHUNK)
        xb = xc[pl.ds(base, CHUNK), :].astype(BF16)
        bb = bc[pl.ds(base, CHUNK), :]
        cb_ = cc[pl.ds(base, CHUNK), :]
        csr8 = cs_t[:, pl.ds(base, CHUNK)]
        dtr8 = dtvt[:, pl.ds(base, CHUNK)]
        cbm = _dot_nt(cb_, bb)
        bt = bb.astype(F32).T
        csc8 = csr8.T
        return base, xb, cb_.astype(F32), csr8, dtr8, cbm, bt, csc8

    def operands(d, st, inp):
        _, xb, cf, csr8, dtr8, cbm, bt, csc8 = inp
        out = []
        for pair in range(2):
            ps = slice(pair * CHUNK, (pair + 1) * CHUNK)
            xp = xb[:, ps]
            sp = st[:, ps]
            spb = sp.astype(BF16)
            per_head = []
            decs = []
            for e in range(2):
                hd = 4 * d + 2 * pair + e
                xk = xp * head_sel[e]
                sk = spb * head_sel[e]
                csb = jnp.broadcast_to(csc8[:, hd:hd + 1], (CHUNK, CHUNK))
                csr = csr8[hd:hd + 1, :]
                dtr = dtr8[hd:hd + 1, :]
                lmat = jnp.exp(jnp.where(masks[d], csb - csr, NEG_BIG))
                mh = (cbm * lmat * dtr).astype(BF16)
                csc = (cf * jnp.exp(csb)).astype(BF16)
                wend = csr8[hd:hd + 1, ends[d]:ends[d] + 1]
                bs = (bt * (jnp.exp(wend - csr) * dtr)).astype(BF16)
                per_head.append((mh, csc, bs, xk, sk))
                decs.append(jnp.exp(wend))
            out.append((per_head, sp, jnp.where(lane_lo, decs[0], decs[1])))
        return out

    def outputs(st, ybuf, base, ops):
        for pair, (per_head, sp, dec) in enumerate(ops):
            ps = slice(pair * CHUNK, (pair + 1) * CHUNK)
            y = sum(_dot(mh, xk) + _dot(csc, sk) for (mh, csc, _, xk, sk) in per_head)
            new_s = sum(_dot(bs, xk) for (_, _, bs, xk, _) in per_head)
            st[:, ps] = sp * dec + new_s
            ybuf[pl.ds(base, CHUNK), ps] = y

    def scan_step(i, carry):
        inp_f = chunk_inputs(i)
        inp_b = chunk_inputs(N_CHUNKS - 1 - i)
        ops_f = operands(0, st_f, inp_f)
        ops_b = operands(1, st_b, inp_b)
        outputs(st_f, ybuf_f, inp_f[0], ops_f)
        outputs(st_b, ybuf_b, inp_b[0], ops_b)
        return carry

    lax.fori_loop(0, N_CHUNKS, scan_step, 0)

    @pl.when(g == 0)
    def _():
        ss_x[...] = jnp.zeros_like(ss_x)
        ss_m[...] = jnp.zeros_like(ss_m)

    def finish(i, carry):
        base = pl.multiple_of(i * CHUNK, CHUNK)
        rows = pl.ds(CHUNK + base, CHUNK)
        y = ybuf_f[rows, :] + ybuf_b[rows, :] + dsk[...] * xc[rows, :]
        yg = y * _silu(za_x[pl.ds(base, CHUNK), :].astype(F32))
        yg_x[pl.ds(base, CHUNK), :] = yg.astype(yg_x.dtype)
        ss_x[pl.ds(base, CHUNK), :] = ss_x[pl.ds(base, CHUNK), :] + jnp.sum(yg * yg, axis=-1, keepdims=True)
        return carry

    lax.fori_loop(0, SEQ // CHUNK, finish, 0)

    y = ybuf_f[PAD_L:CHUNK, :] + ybuf_b[PAD_L:CHUNK, :] + dsk[...] * xc[PAD_L:CHUNK, :]
    yg = y * _silu(za_m[...].astype(F32))
    yg_m[...] = yg.astype(yg_m.dtype)
    ss_m[...] = ss_m[...] + jnp.sum(yg * yg, axis=-1, keepdims=True)


def _ssd(proj_x, proj_m, dtt, conv_w, conv_b, biast, alogt, dsk):
    gw, sn = GROUP_W, SSD_STATE
    nb = BATCH

    def xspec(width, col0):
        return pl.BlockSpec((SEQ, width), lambda b, g, c=col0 // width: (b, c + g))

    def mspec(width, col0):
        return pl.BlockSpec((N_META, width), lambda b, g, c=col0 // width: (0, c + g))

    def wspec(rows, width, col0):
        return pl.BlockSpec((rows, width), lambda b, g, c=col0 // width: (0, c + g))

    in_specs = [
        xspec(gw, COL_ZA), xspec(gw, COL_XS), xspec(sn, COL_B), xspec(sn, COL_C),
        mspec(gw, COL_ZA), mspec(gw, COL_XS), mspec(sn, COL_B), mspec(sn, COL_C),
        pl.BlockSpec((1, 1, 8, T_PAD), lambda b, g: (b, g, 0, 0)),
        wspec(SSD_CONV, gw, 0), wspec(SSD_CONV, sn, SSD_WIDTH), wspec(SSD_CONV, sn, SSD_WIDTH + SSD_GROUPS * sn),
        wspec(1, gw, 0), wspec(1, sn, SSD_WIDTH), wspec(1, sn, SSD_WIDTH + SSD_GROUPS * sn),
        pl.BlockSpec((1, 8, 1), lambda b, g: (g, 0, 0)),
        pl.BlockSpec((1, 8, 1), lambda b, g: (g, 0, 0)),
        pl.BlockSpec((1, gw), lambda b, g: (0, g)),
    ]
    out_shape = [
        jax.ShapeDtypeStruct((nb * SEQ, SSD_WIDTH), BF16),
        jax.ShapeDtypeStruct((nb * N_META, SSD_WIDTH), BF16),
        jax.ShapeDtypeStruct((nb * SEQ, 128), F32),
        jax.ShapeDtypeStruct((nb * N_META, 128), F32),
    ]
    out_specs = [
        pl.BlockSpec((SEQ, gw), lambda b, g: (b, g)),
        pl.BlockSpec((N_META, gw), lambda b, g: (b, g)),
        pl.BlockSpec((SEQ, 128), lambda b, g: (b, 0)),
        pl.BlockSpec((N_META, 128), lambda b, g: (b, 0)),
    ]
    raw_rows = RAW_OFF + T_PAD + 8
    scratch = [
        pltpu.VMEM((raw_rows, gw), F32), pltpu.VMEM((raw_rows, sn), F32), pltpu.VMEM((raw_rows, sn), F32),
        pltpu.VMEM((T_PAD, gw), F32), pltpu.VMEM((T_PAD, sn), BF16), pltpu.VMEM((T_PAD, sn), BF16),
        pltpu.VMEM((T_PAD, gw), F32), pltpu.VMEM((T_PAD, gw), F32),
        pltpu.VMEM((8, T_PAD), F32), pltpu.VMEM((8, T_PAD), F32), pltpu.VMEM((8, T_PAD), F32),
        pltpu.VMEM((sn, gw), F32), pltpu.VMEM((sn, gw), F32),
    ]
    return pl.pallas_call(
        _ssd_kernel,
        grid=(nb, SSD_GROUPS),
        in_specs=in_specs,
        out_specs=out_specs,
        out_shape=out_shape,
        scratch_shapes=scratch,
        compiler_params=_cparams(("parallel", "arbitrary")),
        name="ssd_mixer",
    )(proj_x, proj_x, proj_x, proj_x, proj_m, proj_m, proj_m, proj_m, dtt,
      conv_w, conv_w, conv_w, conv_b, conv_b, conv_b, biast, alogt, dsk)


CONV_TT = 512
CONV_RC = 32
CONV_CC = 512


def _conv_kernel(gv, gg, zb, gv_p, gg_p, gv_n, gg_n, gv_m, gg_m, zb_m, dww, dwb, lng, lnb,
                 yb_x, yb_m, ubuf, cbuf):
    i = pl.program_id(1)
    nt = pl.num_programs(1)
    tt = CONV_TT

    def glu(v_ref, g_ref):
        return v_ref[...].astype(F32) * jax.nn.sigmoid(g_ref[...].astype(F32))

    ubuf[0:16, :] = jnp.zeros((16, CONV_WIDTH), F32)
    ubuf[32:32 + tt, :] = glu(gv, gg)

    @pl.when(i == 0)
    def _():
        ubuf[16:32, :] = glu(gv_m, gg_m)

    @pl.when(i > 0)
    def _():
        ubuf[16:32, :] = glu(gv_p, gg_p)

    @pl.when(i == nt - 1)
    def _():
        ubuf[32 + tt:48 + tt, :] = jnp.zeros((16, CONV_WIDTH), F32)

    @pl.when(i < nt - 1)
    def _():
        ubuf[32 + tt:48 + tt, :] = glu(gv_n, gg_n)

    def conv_rows(win_base, nrows, cs):
        win = ubuf[pl.ds(win_base, nrows + 32), cs]
        acc = jnp.broadcast_to(dwb[:, cs], (nrows, cs.stop - cs.start))
        for k in range(CONV_KERNEL):
            acc = acc + win[k + 1:k + 1 + nrows, :] * dww[k:k + 1, cs]
        return acc

    def norm_gate(c_rows, z_rows):
        mu = jnp.mean(c_rows, axis=-1, keepdims=True)
        var = jnp.mean(jnp.square(c_rows - mu), axis=-1, keepdims=True)
        yn = (c_rows - mu) * lax.rsqrt(var + EPS) * lng[...] + lnb[...]
        return _silu(yn) * _silu(z_rows)

    def chunk(rc, carry):
        base = pl.multiple_of(rc * CONV_RC, CONV_RC)
        for cc in range(CONV_WIDTH // CONV_CC):
            cs = slice(cc * CONV_CC, (cc + 1) * CONV_CC)
            cbuf[pl.ds(base, CONV_RC), cs] = conv_rows(base + 16, CONV_RC, cs)
        out = norm_gate(cbuf[pl.ds(base, CONV_RC), :], zb[pl.ds(base, CONV_RC), :].astype(F32))
        yb_x[pl.ds(base, CONV_RC), :] = out.astype(yb_x.dtype)
        return carry

    lax.fori_loop(0, tt // CONV_RC, chunk, 0)

    @pl.when(i == 0)
    def _():
        for cc in range(CONV_WIDTH // CONV_CC):
            cs = slice(cc * CONV_CC, (cc + 1) * CONV_CC)
            cbuf[0:16, cs] = conv_rows(0, 16, cs)
        out = norm_gate(cbuf[0:16, :], zb_m[...].astype(F32))
        yb_m[...] = out.astype(yb_m.dtype)


def _conformer_conv(proj_x, proj_m, dw_w, dw_b, ln_g, ln_b):
    tt = CONV_TT
    nt = SEQ // tt
    w = CONV_WIDTH
    hb = tt // 16
    last_hb = BATCH * SEQ // 16 - 1

    def main(col0):
        return pl.BlockSpec((tt, w), lambda b, i, c=col0 // w: (b * nt + i, c))

    def prev(col0):
        return pl.BlockSpec((16, w), lambda b, i, c=col0 // w: (jnp.maximum((b * nt + i) * hb - 1, 0), c))

    def nxt(col0):
        return pl.BlockSpec((16, w), lambda b, i, c=col0 // w: (jnp.minimum((b * nt + i + 1) * hb, last_hb), c))

    def meta(col0):
        return pl.BlockSpec((N_META, w), lambda b, i, c=col0 // w: (0, c))

    vec = pl.BlockSpec((1, w), lambda b, i: (0, 0))
    return pl.pallas_call(
        _conv_kernel,
        grid=(BATCH, nt),
        in_specs=[
            main(COL_GV), main(COL_GG), main(COL_ZB),
            prev(COL_GV), prev(COL_GG), nxt(COL_GV), nxt(COL_GG),
            meta(COL_GV), meta(COL_GG), meta(COL_ZB),
            pl.BlockSpec((CONV_KERNEL, w), lambda b, i: (0, 0)), vec, vec, vec,
        ],
        out_specs=[
            pl.BlockSpec((tt, w), lambda b, i: (b * nt + i, 0)),
            pl.BlockSpec((N_META, w), lambda b, i: (b, 0)),
        ],
        out_shape=[
            jax.ShapeDtypeStruct((BATCH * SEQ, w), BF16),
            jax.ShapeDtypeStruct((BATCH * N_META, w), BF16),
        ],
        scratch_shapes=[pltpu.VMEM((tt + 48, w), F32), pltpu.VMEM((tt, w), F32)],
        compiler_params=_cparams(("parallel", "arbitrary")),
        name="conformer_conv",
    )(proj_x, proj_x, proj_x, proj_x, proj_x, proj_x, proj_x, proj_m, proj_m, proj_m,
      dw_w, dw_b.reshape(1, w), ln_g.reshape(1, w), ln_b.reshape(1, w))


NA_WIN = NA_KH * GRID_W


NA_ROWS_PER_TRIP = 4
NA_TBL_W = 1024


def _natten_kernel(q_ref, k_ref, v_ref, z_ref, km_ref, vm_ref, bias_ref, o_ref):
    lane = lax.broadcasted_iota(jnp.int32, (1, 2 * NA_HEAD_DIM), 1)
    lane_lo = lane < NA_HEAD_DIM
    scale = NA_HEAD_DIM ** -0.5
    q_sel = (jnp.where(lane_lo, scale, 0.0).astype(BF16), jnp.where(lane_lo, 0.0, scale).astype(BF16))
    km = km_ref[...]
    vm = vm_ref[...]

    def scores(r):
        rs = jnp.clip(r - NA_KH // 2, 0, GRID_ROWS - NA_KH)
        cls = rs - r + NA_KH - 1
        par = lax.rem(cls, 2)
        boff = pl.multiple_of((cls - par) * GRID_W, 2 * GRID_W)
        qs = pl.multiple_of(r * GRID_W, GRID_W)
        ks = pl.multiple_of(rs * GRID_W, GRID_W)
        q = q_ref[pl.ds(qs, GRID_W), :]
        kw = k_ref[pl.ds(ks, NA_WIN), :]
        q2 = jnp.concatenate([q * q_sel[0], q * q_sel[1]], axis=0)
        s = _dot_nt(q2, kw) + bias_ref[0, par, :, pl.ds(boff, NA_WIN)]
        sm = _dot_nt(q2, km)
        return qs, ks, s, sm

    def softmax(s, sm):
        m = jnp.maximum(jnp.max(s, axis=-1, keepdims=True), jnp.max(sm, axis=-1, keepdims=True))
        p = jnp.exp(s - m)
        pm = jnp.exp(sm - m)
        den = jnp.sum(p, axis=-1, keepdims=True) + jnp.sum(pm, axis=-1, keepdims=True)
        return p.astype(BF16), pm.astype(BF16), den

    def finish(qs, ks, p, pm, den):
        o2 = (_dot(p, v_ref[pl.ds(ks, NA_WIN), :]) + _dot(pm, vm)) / den
        o = jnp.where(lane_lo, o2[:GRID_W], o2[GRID_W:])
        o = o * _silu(z_ref[pl.ds(qs, GRID_W), :].astype(F32))
        o_ref[pl.ds(qs, GRID_W), :] = o.astype(o_ref.dtype)

    def trip(i, carry):
        rows = [scores(i * NA_ROWS_PER_TRIP + u) for u in range(NA_ROWS_PER_TRIP)]
        probs = [softmax(s, sm) for (_, _, s, sm) in rows]
        for (qs, ks, _, _), (p, pm, den) in zip(rows, probs):
            finish(qs, ks, p, pm, den)
        return carry

    lax.fori_loop(0, GRID_ROWS // NA_ROWS_PER_TRIP, trip, 0)


def _natten(proj_x, proj_m, bias_tbl):
    pw = 2 * NA_HEAD_DIM
    npair = NA_HEADS // 2

    def xspec(col0):
        return pl.BlockSpec((SEQ, pw), lambda b, h, c=col0 // pw: (b, c + h))

    def mspec(col0):
        return pl.BlockSpec((N_META, pw), lambda b, h, c=col0 // pw: (b, c + h))

    return pl.pallas_call(
        _natten_kernel,
        grid=(BATCH, npair),
        in_specs=[
            xspec(0), xspec(NA_WIDTH), xspec(2 * NA_WIDTH), xspec(3 * NA_WIDTH),
            mspec(NA_WIDTH), mspec(2 * NA_WIDTH),
            pl.BlockSpec((1, 2, 2 * GRID_W, NA_TBL_W), lambda b, h: (h, 0, 0, 0)),
        ],
        out_specs=pl.BlockSpec((SEQ, pw), lambda b, h: (b, h)),
        out_shape=jax.ShapeDtypeStruct((BATCH * SEQ, NA_WIDTH), BF16),
        compiler_params=_cparams(("parallel", "arbitrary")),
        name="natten",
    )(proj_x, proj_x, proj_x, proj_x, proj_m, proj_m, bias_tbl)


def _natten_bias_table(rpb):
    col = np.arange(GRID_W)
    col_start = np.clip(col - NA_KW // 2, 0, GRID_W - NA_KW)
    col_mask = (col[None, :] >= col_start[:, None]) & (col[None, :] < col_start[:, None] + NA_KW)
    col_idx = np.clip(col[None, :] - col[:, None] + NA_KW - 1, 0, 2 * NA_KW - 2)
    bias_cols = jnp.where(col_mask[None, None], rpb[:, :, col_idx].astype(F32), NEG_BIG)
    n_dr = 2 * NA_KH - 1
    t = jnp.transpose(bias_cols, (0, 2, 1, 3)).reshape(NA_HEADS, GRID_W, n_dr * GRID_W)
    t = jnp.pad(t, ((0, 0), (0, 0), (0, NA_TBL_W + GRID_W - n_dr * GRID_W)))
    t = jnp.stack([t[:, :, :NA_TBL_W], t[:, :, GRID_W:]], axis=1)
    t = t.reshape(NA_HEADS // 2, 2, 2, GRID_W, NA_TBL_W)
    return jnp.transpose(t, (0, 2, 1, 3, 4)).reshape(NA_HEADS // 2, 2, 2 * GRID_W, NA_TBL_W)


def kernel(x, meta_tokens, e_norm_g, e_w_in, e_conv_w, e_conv_b, e_dt_bias, e_A_log, e_D, e_ssd_norm_g, e_dw_w, e_dw_b, e_ln_g, e_ln_b, e_w_out, o_norm_g, o_w_in, o_rpb, o_w_out, final_norm_g):
    nb = x.shape[0]
    x2d = x.reshape(nb * SEQ, D_MODEL)
    meta = meta_tokens.astype(x.dtype)

    w_in_t = jnp.swapaxes(e_w_in, 1, 2)[0]
    dt0 = COL_ZB
    w_main = jnp.concatenate([w_in_t[:dt0], w_in_t[dt0 + 2 * SSD_HEADS:]], axis=0).astype(BF16)
    w_dt = jnp.pad(w_in_t[dt0:dt0 + 2 * SSD_HEADS].T, ((0, 0), (0, DT_COLS - 2 * SSD_HEADS)))
    w_dt_hi = w_dt.astype(BF16)
    w_dt_lo = (w_dt - w_dt_hi.astype(F32)).astype(BF16)
    wdt = jnp.concatenate([w_dt_hi, w_dt_lo], axis=1)

    proj_x, dt_x = _in_proj(x2d, e_norm_g[0], w_main, wdt, w_is_nk=True, name="in_proj0")
    proj_m, dt_m = _in_proj(meta, e_norm_g[0], w_main, wdt, w_is_nk=True, name="in_proj0_meta")

    def group_dt(a, rows):
        a = a[:, :2 * SSD_HEADS].reshape(-1, rows, 2, SSD_GROUPS, HEADS_PER_GROUP)
        return jnp.transpose(a, (0, 3, 2, 4, 1)).reshape(-1, SSD_GROUPS, 2 * HEADS_PER_GROUP, rows)

    dt_seq_t = jnp.concatenate([
        jnp.full((nb, SSD_GROUPS, 2 * HEADS_PER_GROUP, PAD_L), NEG_BIG, F32),
        jnp.broadcast_to(group_dt(dt_m, N_META), (nb, SSD_GROUPS, 2 * HEADS_PER_GROUP, N_META)),
        group_dt(dt_x, SEQ),
    ], axis=3)

    def group_vec(a):
        return jnp.transpose(a.reshape(2, SSD_GROUPS, HEADS_PER_GROUP), (1, 0, 2)).reshape(SSD_GROUPS, -1)

    bias_g = group_vec(e_dt_bias[0].astype(F32))
    alog_g = group_vec(e_A_log[0].astype(F32))
    dsk = jnp.repeat(e_D[0].astype(F32), SSD_HEAD_DIM).reshape(1, SSD_WIDTH)

    yg_x, yg_m, ss_x, ss_m = _ssd(
        proj_x, proj_m, dt_seq_t, e_conv_w[0], e_conv_b[0].reshape(1, -1),
        bias_g[:, :, None], alog_g[:, :, None], dsk)
    yb_x, yb_m = _conformer_conv(proj_x, proj_m, e_dw_w[0], e_dw_b[0], e_ln_g[0], e_ln_b[0])

    w_out = e_w_out[0].astype(BF16)
    wa, wb = w_out[:SSD_WIDTH], w_out[SSD_WIDTH:]
    h1_x = _out_proj0(x2d, yg_x, ss_x, e_ssd_norm_g[0], yb_x, wa, wb, name="out_proj0")
    meta_b = jnp.broadcast_to(meta[None], (nb, N_META, D_MODEL)).reshape(nb * N_META, D_MODEL)
    h1_m = _out_proj0(meta_b, yg_m, ss_m, e_ssd_norm_g[0], yb_m, wa, wb, name="out_proj0_meta")

    w_in1 = o_w_in[0].astype(BF16)
    p1_x = _in_proj(h1_x, o_norm_g[0], w_in1, name="in_proj1")
    p1_m = _in_proj(h1_m, o_norm_g[0], w_in1, name="in_proj1_meta")
    o_x = _natten(p1_x, p1_m, _natten_bias_table(o_rpb[0]))
    out = _out_proj1(h1_x, o_x, o_w_out[0].astype(BF16), final_norm_g, name="out_proj1")
    return out.reshape(nb, SEQ, D_MODEL)
```

```python
import functools

import jax
import jax.numpy as jnp
import numpy as np
from jax import lax
from jax.experimental import pallas as pl
from jax.experimental.pallas import tpu as pltpu

F32 = jnp.float32
BF16 = jnp.bfloat16

D_MODEL = 2048
BATCH = 4
SEQ = 2048
N_META = 16
GRID_W = 64
EPS = 1e-5

SSD_HEAD_DIM = 64
SSD_WIDTH = 2048
SSD_HEADS = 32
SSD_GROUPS = 8
SSD_STATE = 128
SSD_CONV = 5
CHUNK = 128
HEADS_PER_GROUP = SSD_HEADS // SSD_GROUPS
GROUP_W = HEADS_PER_GROUP * SSD_HEAD_DIM

CONV_WIDTH = 2048
CONV_KERNEL = 31
CONV_HALO = CONV_KERNEL // 2

NA_HEAD_DIM = 64
NA_WIDTH = 2048
NA_HEADS = 32
NA_KH = 8
NA_KW = 16
GRID_ROWS = SEQ // GRID_W

PAD_L = (-(N_META + SEQ)) % CHUNK
T_PAD = PAD_L + N_META + SEQ
N_CHUNKS = T_PAD // CHUNK
RAW_OFF = 8
SSD_CHUNKS_PER_TRIP = 2
NEG_BIG = -1e30

VMEM_LIMIT_BYTES = 56 * 1024 * 1024

COL_ZA = 0
COL_XS = 2048
COL_B = 4096
COL_C = 5120
COL_ZB = 6144
COL_GV = 8192
COL_GG = 10240
N_MAIN = 12288
DT_COLS = 128


def _cparams(sem):
    return pltpu.CompilerParams(dimension_semantics=sem, vmem_limit_bytes=VMEM_LIMIT_BYTES)


def _dot(a, b):
    return jnp.dot(a, b, preferred_element_type=F32)


def _dot_nt(a, b):
    return lax.dot_general(a, b, (((1,), (1,)), ((), ())), preferred_element_type=F32)


def _silu(x):
    return x * jax.nn.sigmoid(x)


def _softplus(x):
    return jnp.maximum(x, 0.0) + jnp.log1p(jnp.exp(-jnp.abs(x)))


def _split3(v):
    hi = v.astype(BF16)
    r1 = v - hi.astype(F32)
    mid = r1.astype(BF16)
    lo = (r1 - mid.astype(F32)).astype(BF16)
    return hi, mid, lo


W_PREP_ROWS = 1024


def _w_prep_kernel(w_ref, wdt_ref, o_ref, odt_ref):
    o_ref[...] = w_ref[...].astype(BF16)

    @pl.when(pl.program_id(0) == 0)
    def _():
        w = wdt_ref[...]
        hi = w.astype(BF16)
        lo = (w - hi.astype(F32)).astype(BF16)
        pad = jnp.zeros((DT_COLS - 2 * SSD_HEADS, w.shape[1]), BF16)
        odt_ref[...] = jnp.concatenate([hi, pad, lo, pad], axis=0)


def _w_prep(w_t):
    n_all, d = w_t.shape
    ndt = 2 * SSD_HEADS
    dt0 = COL_ZB
    tr = W_PREP_ROWS
    return pl.pallas_call(
        _w_prep_kernel,
        grid=(N_MAIN // tr,),
        in_specs=[
            pl.BlockSpec((pl.Element(tr), pl.Element(d)),
                         lambda j: ((j * (tr // ndt) + jnp.where(j >= dt0 // tr, 1, 0)) * ndt, 0)),
            pl.BlockSpec((ndt, d), lambda j: (dt0 // ndt, 0)),
        ],
        out_specs=[
            pl.BlockSpec((tr, d), lambda j: (j, 0)),
            pl.BlockSpec((2 * DT_COLS, d), lambda j: (0, 0)),
        ],
        out_shape=[
            jax.ShapeDtypeStruct((N_MAIN, d), BF16),
            jax.ShapeDtypeStruct((2 * DT_COLS, d), BF16),
        ],
        compiler_params=_cparams(("arbitrary",)),
        name="w_in_prep",
    )(w_t, w_t)


def _in_proj_kernel(x_ref, g_ref, w_ref, *rest, with_dt, w_is_nk):
    if with_dt:
        wdt_ref, o_ref, dt_ref, u_ref = rest
    else:
        o_ref, u_ref = rest

    @pl.when(pl.program_id(1) == 0)
    def _():
        x = x_ref[...]
        ms = jnp.mean(x * x, axis=-1, keepdims=True)
        y = x * lax.rsqrt(ms + EPS) * g_ref[...]
        u = y.astype(BF16)
        u_ref[...] = u
        if with_dt:
            u_lo = (y - u.astype(F32)).astype(BF16)
            w_hi = wdt_ref[:DT_COLS, :]
            w_lo = wdt_ref[DT_COLS:, :]
            dt_ref[...] = _dot_nt(u, w_hi) + _dot_nt(u_lo, w_hi) + _dot_nt(u, w_lo)

    mm = _dot_nt if w_is_nk else _dot
    o_ref[...] = mm(u_ref[...], w_ref[...]).astype(o_ref.dtype)


def _in_proj(x2d, g, w, wdt=None, *, w_is_nk=False, name):
    m, d = x2d.shape
    n = w.shape[0] if w_is_nk else w.shape[1]
    tm = min(m, 1024)
    tn = 1024
    with_dt = wdt is not None
    in_specs = [
        pl.BlockSpec((tm, d), lambda i, j: (i, 0)),
        pl.BlockSpec((1, d), lambda i, j: (0, 0)),
        pl.BlockSpec((tn, d), lambda i, j: (j, 0)) if w_is_nk else pl.BlockSpec((d, tn), lambda i, j: (0, j)),
    ]
    out_shape = [jax.ShapeDtypeStruct((m, n), BF16)]
    out_specs = [pl.BlockSpec((tm, tn), lambda i, j: (i, j))]
    args = [x2d, g.reshape(1, d), w]
    if with_dt:
        in_specs.append(pl.BlockSpec((2 * DT_COLS, d), lambda i, j: (0, 0)))
        out_shape.append(jax.ShapeDtypeStruct((m, DT_COLS), F32))
        out_specs.append(pl.BlockSpec((tm, DT_COLS), lambda i, j: (i, 0)))
        args.append(wdt)
    res = pl.pallas_call(
        functools.partial(_in_proj_kernel, with_dt=with_dt, w_is_nk=w_is_nk),
        grid=(m // tm, n // tn),
        in_specs=in_specs,
        out_specs=out_specs,
        out_shape=out_shape,
        scratch_shapes=[pltpu.VMEM((tm, d), BF16)],
        compiler_params=_cparams(("parallel", "arbitrary")),
        name=name,
    )(*args)
    return res if with_dt else res[0]


def _out_proj0_kernel(h_ref, ya_ref, ss_ref, ng_ref, yb_ref, wa_ref, wb_ref, o_ref, yan_ref):
    @pl.when(pl.program_id(1) == 0)
    def _():
        r = lax.rsqrt(ss_ref[:, 0:1] * (1.0 / SSD_WIDTH) + EPS)
        yan_ref[...] = (ya_ref[...].astype(F32) * r * ng_ref[...]).astype(BF16)

    o_ref[...] = h_ref[...] + _dot(yan_ref[...], wa_ref[...]) + _dot(yb_ref[...], wb_ref[...])


def _out_proj0(h2d, ya, ss, ng, yb, wa, wb, *, name):
    m, d = h2d.shape
    tm = min(m, 1024)
    tn = 512
    return pl.pallas_call(
        _out_proj0_kernel,
        grid=(m // tm, d // tn),
        in_specs=[
            pl.BlockSpec((tm, tn), lambda i, j: (i, j)),
            pl.BlockSpec((tm, SSD_WIDTH), lambda i, j: (i, 0)),
            pl.BlockSpec((tm, 128), lambda i, j: (i, 0)),
            pl.BlockSpec((1, SSD_WIDTH), lambda i, j: (0, 0)),
            pl.BlockSpec((tm, CONV_WIDTH), lambda i, j: (i, 0)),
            pl.BlockSpec((SSD_WIDTH, tn), lambda i, j: (0, j)),
            pl.BlockSpec((CONV_WIDTH, tn), lambda i, j: (0, j)),
        ],
        out_specs=pl.BlockSpec((tm, tn), lambda i, j: (i, j)),
        out_shape=jax.ShapeDtypeStruct((m, d), F32),
        scratch_shapes=[pltpu.VMEM((tm, SSD_WIDTH), BF16)],
        compiler_params=_cparams(("parallel", "arbitrary")),
        name=name,
    )(h2d, ya, ss, ng.reshape(1, SSD_WIDTH), yb, wa, wb)


def _out_proj1_kernel(h_ref, o_ref, w_ref, g_ref, out_ref):
    hn = h_ref[...] + _dot(o_ref[...], w_ref[...])
    ms = jnp.mean(hn * hn, axis=-1, keepdims=True)
    out_ref[...] = hn * lax.rsqrt(ms + EPS) * g_ref[...]


def _out_proj1(h2d, o, w, g, *, name):
    m, d = h2d.shape
    tm = 512
    return pl.pallas_call(
        _out_proj1_kernel,
        grid=(m // tm,),
        in_specs=[
            pl.BlockSpec((tm, d), lambda i: (i, 0)),
            pl.BlockSpec((tm, NA_WIDTH), lambda i: (i, 0)),
            pl.BlockSpec((NA_WIDTH, d), lambda i: (0, 0)),
            pl.BlockSpec((1, d), lambda i: (0, 0)),
        ],
        out_specs=pl.BlockSpec((tm, d), lambda i: (i, 0)),
        out_shape=jax.ShapeDtypeStruct((m, d), F32),
        compiler_params=_cparams(("parallel",)),
        name=name,
    )(h2d, o, w, g.reshape(1, d))


def _ssd_kernel(za_x, xs_x, bm_x, cm_x, za_m, xs_m, bm_m, cm_m, dtt_ref,
                cwx, cwb, cwc, cbx, cbb, cbc, biast, alogt, dsk,
                yg_x, yg_m, ss_x, ss_m,
                raw_x, raw_b, raw_c, xc, bc, cc, ybuf_f, ybuf_b, dtvt, at, cs_t, st_f, st_b):
    g = pl.program_id(1)

    for raw, m_ref, x_ref in ((raw_x, xs_m, xs_x), (raw_b, bm_m, bm_x), (raw_c, cm_m, cm_x)):
        w = raw.shape[1]
        raw[0:RAW_OFF + PAD_L, :] = jnp.zeros((RAW_OFF + PAD_L, w), F32)
        raw[RAW_OFF + PAD_L:RAW_OFF + CHUNK, :] = m_ref[...].astype(F32)
        raw[RAW_OFF + CHUNK:RAW_OFF + T_PAD, :] = x_ref[...].astype(F32)
        raw[RAW_OFF + T_PAD:RAW_OFF + T_PAD + 8, :] = jnp.zeros((8, w), F32)

    row = lax.broadcasted_iota(jnp.int32, (CHUNK, 1), 0)

    def conv_chunk(c, carry):
        base = pl.multiple_of(c * CHUNK, CHUNK)
        keep = jnp.logical_or(c > 0, row >= PAD_L)
        for raw, cw, cb, dst in ((raw_x, cwx, cbx, xc), (raw_b, cwb, cbb, bc), (raw_c, cwc, cbc, cc)):
            win = raw[pl.ds(base, CHUNK + 16), :]
            acc = cb[...]
            for k in range(SSD_CONV):
                s0 = RAW_OFF - SSD_CONV // 2 + k
                acc = acc + win[s0:s0 + CHUNK, :] * cw[k:k + 1, :]
            v = jnp.where(keep, _silu(acc), 0.0)
            dst[pl.ds(base, CHUNK), :] = v.astype(dst.dtype)
        return carry

    lax.fori_loop(0, N_CHUNKS, conv_chunk, 0)

    a_t = -jnp.exp(alogt[0])
    dtv = _softplus(dtt_ref[0, 0] + biast[0])
    dtvt[...] = dtv
    at[...] = dtv * a_t

    li = lax.broadcasted_iota(jnp.int32, (CHUNK, CHUNK), 0)
    si = lax.broadcasted_iota(jnp.int32, (CHUNK, CHUNK), 1)
    lane = lax.broadcasted_iota(jnp.int32, (1, CHUNK), 1)
    lane_lo = lane < SSD_HEAD_DIM
    head_sel = (jnp.where(lane_lo, 1.0, 0.0).astype(BF16), jnp.where(lane_lo, 0.0, 1.0).astype(BF16))
    masks = (li >= si, li <= si)
    ends = (CHUNK - 1, 0)

    a2 = jnp.concatenate([at[:, c * CHUNK:(c + 1) * CHUNK] for c in range(N_CHUNKS)]
                         + [jnp.zeros((8, CHUNK), F32)], axis=0)
    pieces = _split3(a2)
    t_f = jnp.where(li <= si, 1.0, 0.0).astype(BF16)
    t_b = jnp.where(li >= si, 1.0, 0.0).astype(BF16)
    cs_f = sum(_dot(p, t_f) for p in pieces)
    cs_b = sum(_dot(p, t_b) for p in pieces)
    is_fwd = lax.rem(lax.broadcasted_iota(jnp.int32, (8 * (N_CHUNKS + 1), 1), 0), 8) < HEADS_PER_GROUP
    cs2 = jnp.where(is_fwd, cs_f, cs_b)
    for c in range(N_CHUNKS):
        cs_t[:, c * CHUNK:(c + 1) * CHUNK] = cs2[8 * c:8 * c + 8]

    st_f[...] = jnp.zeros_like(st_f)
    st_b[...] = jnp.zeros_like(st_b)

    def chunk_inputs(c):
        base = c * CHUNK if isinstance(c, int) else pl.multiple_of(c * CHUNK, CHUNK)
        xb = xc[pl.ds(base, CHUNK), :].astype(BF16)
        bb = bc[pl.ds(base, CHUNK), :]
        cb_ = cc[pl.ds(base, CHUNK), :]
        csr8 = cs_t[:, pl.ds(base, CHUNK)]
        dtr8 = dtvt[:, pl.ds(base, CHUNK)]
        cbm = _dot_nt(cb_, bb)
        bt = bb.astype(F32).T
        csc8 = csr8.T
        return base, xb, cb_.astype(F32), csr8, dtr8, cbm, bt, csc8

    def operands(d, inp):
        _, xb, cf, csr8, dtr8, cbm, bt, csc8 = inp
        out = []
        for pair in range(2):
            xp = xb[:, pair * CHUNK:(pair + 1) * CHUNK]
            per_head = []
            decs = []
            for e in range(2):
                hd = 4 * d + 2 * pair + e
                xk = xp * head_sel[e]
                csb = jnp.broadcast_to(csc8[:, hd:hd + 1], (CHUNK, CHUNK))
                csr = csr8[hd:hd + 1, :]
                dtr = dtr8[hd:hd + 1, :]
                lmat = jnp.exp(jnp.where(masks[d], csb - csr, NEG_BIG))
                mh = (cbm * lmat * dtr).astype(BF16)
                csc = (cf * jnp.exp(csb)).astype(BF16)
                wend = csr8[hd:hd + 1, ends[d]:ends[d] + 1]
                bs = (bt * (jnp.exp(wend - csr) * dtr)).astype(BF16)
                per_head.append((mh, csc, bs, xk))
                decs.append(jnp.exp(wend))
            out.append((per_head, jnp.where(lane_lo, decs[0], decs[1])))
        return out

    def outputs(state, ybuf, base, ops):
        new_state = []
        for pair, (per_head, dec) in enumerate(ops):
            ps = slice(pair * CHUNK, (pair + 1) * CHUNK)
            sp = state[:, ps]
            spb = sp.astype(BF16)
            y = sum(_dot(mh, xk) + _dot(csc, spb * head_sel[e]) for e, (mh, csc, _, xk) in enumerate(per_head))
            new_s = sum(_dot(bs, xk) for (_, _, bs, xk) in per_head)
            ybuf[pl.ds(base, CHUNK), ps] = y
            new_state.append(sp * dec + new_s)
        return jnp.concatenate(new_state, axis=1)

    def scan_trip(chunks_f, chunks_b):
        inps_f = [chunk_inputs(c) for c in chunks_f]
        inps_b = [chunk_inputs(c) for c in chunks_b]
        ops_f = [operands(0, inp) for inp in inps_f]
        ops_b = [operands(1, inp) for inp in inps_b]
        sf = st_f[...]
        sb = st_b[...]
        for inp_f, op_f, inp_b, op_b in zip(inps_f, ops_f, inps_b, ops_b):
            sf = outputs(sf, ybuf_f, inp_f[0], op_f)
            sb = outputs(sb, ybuf_b, inp_b[0], op_b)
        st_f[...] = sf
        st_b[...] = sb

    def scan_step(i, carry):
        first = i * SSD_CHUNKS_PER_TRIP
        scan_trip([first + u for u in range(SSD_CHUNKS_PER_TRIP)],
                  [N_CHUNKS - 1 - first - u for u in range(SSD_CHUNKS_PER_TRIP)])
        return carry

    n_trips = N_CHUNKS // SSD_CHUNKS_PER_TRIP
    lax.fori_loop(0, n_trips, scan_step, 0)
    for c in range(n_trips * SSD_CHUNKS_PER_TRIP, N_CHUNKS):
        scan_trip([c], [N_CHUNKS - 1 - c])

    @pl.when(g == 0)
    def _():
        ss_x[...] = jnp.zeros_like(ss_x)
        ss_m[...] = jnp.zeros_like(ss_m)

    def finish(i, carry):
        base = pl.multiple_of(i * CHUNK, CHUNK)
        rows = pl.ds(CHUNK + base, CHUNK)
        y = ybuf_f[rows, :] + ybuf_b[rows, :] + dsk[...] * xc[rows, :]
        yg = y * _silu(za_x[pl.ds(base, CHUNK), :].astype(F32))
        yg_x[pl.ds(base, CHUNK), :] = yg.astype(yg_x.dtype)
        ss_x[pl.ds(base, CHUNK), :] = ss_x[pl.ds(base, CHUNK), :] + jnp.sum(yg * yg, axis=-1, keepdims=True)
        return carry

    lax.fori_loop(0, SEQ // CHUNK, finish, 0)

    y = ybuf_f[PAD_L:CHUNK, :] + ybuf_b[PAD_L:CHUNK, :] + dsk[...] * xc[PAD_L:CHUNK, :]
    yg = y * _silu(za_m[...].astype(F32))
    yg_m[...] = yg.astype(yg_m.dtype)
    ss_m[...] = ss_m[...] + jnp.sum(yg * yg, axis=-1, keepdims=True)


def _ssd(proj_x, proj_m, dtt, conv_w, conv_b, biast, alogt, dsk):
    gw, sn = GROUP_W, SSD_STATE
    nb = BATCH

    def xspec(width, col0):
        return pl.BlockSpec((SEQ, width), lambda b, g, c=col0 // width: (b, c + g))

    def mspec(width, col0):
        return pl.BlockSpec((N_META, width), lambda b, g, c=col0 // width: (0, c + g))

    def wspec(rows, width, col0):
        return pl.BlockSpec((rows, width), lambda b, g, c=col0 // width: (0, c + g))

    in_specs = [
        xspec(gw, COL_ZA), xspec(gw, COL_XS), xspec(sn, COL_B), xspec(sn, COL_C),
        mspec(gw, COL_ZA), mspec(gw, COL_XS), mspec(sn, COL_B), mspec(sn, COL_C),
        pl.BlockSpec((1, 1, 8, T_PAD), lambda b, g: (b, g, 0, 0)),
        wspec(SSD_CONV, gw, 0), wspec(SSD_CONV, sn, SSD_WIDTH), wspec(SSD_CONV, sn, SSD_WIDTH + SSD_GROUPS * sn),
        wspec(1, gw, 0), wspec(1, sn, SSD_WIDTH), wspec(1, sn, SSD_WIDTH + SSD_GROUPS * sn),
        pl.BlockSpec((1, 8, 1), lambda b, g: (g, 0, 0)),
        pl.BlockSpec((1, 8, 1), lambda b, g: (g, 0, 0)),
        pl.BlockSpec((1, gw), lambda b, g: (0, g)),
    ]
    out_shape = [
        jax.ShapeDtypeStruct((nb * SEQ, SSD_WIDTH), BF16),
        jax.ShapeDtypeStruct((nb * N_META, SSD_WIDTH), BF16),
        jax.ShapeDtypeStruct((nb * SEQ, 128), F32),
        jax.ShapeDtypeStruct((nb * N_META, 128), F32),
    ]
    out_specs = [
        pl.BlockSpec((SEQ, gw), lambda b, g: (b, g)),
        pl.BlockSpec((N_META, gw), lambda b, g: (b, g)),
        pl.BlockSpec((SEQ, 128), lambda b, g: (b, 0)),
        pl.BlockSpec((N_META, 128), lambda b, g: (b, 0)),
    ]
    raw_rows = RAW_OFF + T_PAD + 8
    scratch = [
        pltpu.VMEM((raw_rows, gw), F32), pltpu.VMEM((raw_rows, sn), F32), pltpu.VMEM((raw_rows, sn), F32),
        pltpu.VMEM((T_PAD, gw), F32), pltpu.VMEM((T_PAD, sn), BF16), pltpu.VMEM((T_PAD, sn), BF16),
        pltpu.VMEM((T_PAD, gw), F32), pltpu.VMEM((T_PAD, gw), F32),
        pltpu.VMEM((8, T_PAD), F32), pltpu.VMEM((8, T_PAD), F32), pltpu.VMEM((8, T_PAD), F32),
        pltpu.VMEM((sn, gw), F32), pltpu.VMEM((sn, gw), F32),
    ]
    return pl.pallas_call(
        _ssd_kernel,
        grid=(nb, SSD_GROUPS),
        in_specs=in_specs,
        out_specs=out_specs,
        out_shape=out_shape,
        scratch_shapes=scratch,
        compiler_params=_cparams(("parallel", "arbitrary")),
        name="ssd_mixer",
    )(proj_x, proj_x, proj_x, proj_x, proj_m, proj_m, proj_m, proj_m, dtt,
      conv_w, conv_w, conv_w, conv_b, conv_b, conv_b, biast, alogt, dsk)


CONV_TT = 512
CONV_RC = 32
CONV_CC = 256
CONV_LW = 256
CONV_D0 = 1
CONV_SB = 64
CONV_SROWS = 576
CONV_UROWS = CONV_SROWS + 8


def _conv_kernel(gv, gg, zb, gv_p, gg_p, gv_n, gg_n, gv_m, gg_m, zb_m, dww, dwb, lng, lnb,
                 yb_x, yb_m, ubuf, sh, cbuf, cbuf_m, wrep):
    i = pl.program_id(1)
    nt = pl.num_programs(1)
    tt = CONV_TT

    def glu(v_ref, g_ref):
        return v_ref[...].astype(F32) * jax.nn.sigmoid(g_ref[...].astype(F32))

    ubuf[0:16, :] = jnp.zeros((16, CONV_WIDTH), F32)
    ubuf[48 + tt:, :] = jnp.zeros((CONV_UROWS - 48 - tt, CONV_WIDTH), F32)
    ubuf[32:32 + tt, :] = glu(gv, gg)

    @pl.when(i == 0)
    def _():
        ubuf[16:32, :] = glu(gv_m, gg_m)

    @pl.when(i > 0)
    def _():
        ubuf[16:32, :] = glu(gv_p, gg_p)

    @pl.when(i == nt - 1)
    def _():
        ubuf[32 + tt:48 + tt, :] = jnp.zeros((16, CONV_WIDTH), F32)

    @pl.when(i < nt - 1)
    def _():
        ubuf[32 + tt:48 + tt, :] = glu(gv_n, gg_n)

    def conv_rows(row0, nrows, cs):
        outs = []
        for l0 in range(0, cs.stop - cs.start, CONV_LW):
            ls = slice(l0, l0 + CONV_LW)
            gs = slice(cs.start + l0, cs.start + l0 + CONV_LW)
            acc = jnp.broadcast_to(dwb[:, gs], (nrows, CONV_LW))
            for r in range(8):
                qs = [q for q in range(5) if 0 <= 8 * q + r - CONV_D0 < CONV_KERNEL]
                span = pl.ds(row0 + 8 * qs[0], 8 * (qs[-1] - qs[0]) + nrows)
                blk = ubuf[span, gs] if r == 0 else sh[r - 1, span, ls]
                for q in qs:
                    k = 8 * q + r - CONV_D0
                    tap = blk[8 * (q - qs[0]):8 * (q - qs[0]) + nrows, :]
                    acc = acc + tap * pltpu.repeat(wrep[k, :, gs], nrows // 8, axis=0)
            outs.append(acc)
        return jnp.concatenate(outs, axis=1)

    def norm_gate(c_rows, z_rows):
        n = c_rows.shape[-1]
        mu = jnp.sum(c_rows, axis=-1, keepdims=True) * (1.0 / n)
        var = jnp.sum(c_rows * c_rows, axis=-1, keepdims=True) * (1.0 / n) - mu * mu
        yn = (c_rows - mu) * lax.rsqrt(var + EPS) * lng[...] + lnb[...]
        return _silu(yn) * _silu(z_rows)

    for k in range(CONV_KERNEL):
        wrep[k] = jnp.broadcast_to(dww[k:k + 1, :], (8, CONV_WIDTH))

    for cc in range(CONV_WIDTH // CONV_CC):
        cs = slice(cc * CONV_CC, (cc + 1) * CONV_CC)

        def shift_block(rb, carry, cs=cs):
            base = pl.multiple_of(rb * CONV_SB, CONV_SB)
            win = ubuf[pl.ds(base, CONV_SB + 8), cs]
            for r in range(1, 8):
                sh[r - 1, pl.ds(base, CONV_SB), :] = win[r:r + CONV_SB, :]
            return carry

        lax.fori_loop(0, CONV_SROWS // CONV_SB, shift_block, 0)

        def conv_chunk(rc, carry, cs=cs):
            base = pl.multiple_of(rc * CONV_RC, CONV_RC)
            cbuf[pl.ds(base, CONV_RC), cs] = conv_rows(base + 16, CONV_RC, cs)
            return carry

        lax.fori_loop(0, tt // CONV_RC, conv_chunk, 0)

        @pl.when(i == 0)
        def _(cs=cs):
            cbuf_m[:, cs] = conv_rows(0, N_META, cs)

    def gate_chunk(rc, carry):
        for half in range(2):
            base = pl.multiple_of(rc * CONV_RC + half * (CONV_RC // 2), CONV_RC // 2)
            rows = pl.ds(base, CONV_RC // 2)
            out = norm_gate(cbuf[rows, :], zb[rows, :].astype(F32))
            yb_x[rows, :] = out.astype(yb_x.dtype)
        return carry

    lax.fori_loop(0, tt // CONV_RC, gate_chunk, 0)

    @pl.when(i == 0)
    def _():
        out = norm_gate(cbuf_m[...], zb_m[...].astype(F32))
        yb_m[...] = out.astype(yb_m.dtype)


def _conformer_conv(proj_x, proj_m, dw_w, dw_b, ln_g, ln_b):
    tt = CONV_TT
    nt = SEQ // tt
    w = CONV_WIDTH
    hb = tt // 16
    last_hb = BATCH * SEQ // 16 - 1

    def main(col0):
        return pl.BlockSpec((tt, w), lambda b, i, c=col0 // w: (b * nt + i, c))

    def prev(col0):
        return pl.BlockSpec((16, w), lambda b, i, c=col0 // w: (jnp.maximum((b * nt + i) * hb - 1, 0), c))

    def nxt(col0):
        return pl.BlockSpec((16, w), lambda b, i, c=col0 // w: (jnp.minimum((b * nt + i + 1) * hb, last_hb), c))

    def meta(col0):
        return pl.BlockSpec((N_META, w), lambda b, i, c=col0 // w: (0, c))

    vec = pl.BlockSpec((1, w), lambda b, i: (0, 0))
    return pl.pallas_call(
        _conv_kernel,
        grid=(BATCH, nt),
        in_specs=[
            main(COL_GV), main(COL_GG), main(COL_ZB),
            prev(COL_GV), prev(COL_GG), nxt(COL_GV), nxt(COL_GG),
            meta(COL_GV), meta(COL_GG), meta(COL_ZB),
            pl.BlockSpec((CONV_KERNEL, w), lambda b, i: (0, 0)), vec, vec, vec,
        ],
        out_specs=[
            pl.BlockSpec((tt, w), lambda b, i: (b * nt + i, 0)),
            pl.BlockSpec((N_META, w), lambda b, i: (b, 0)),
        ],
        out_shape=[
            jax.ShapeDtypeStruct((BATCH * SEQ, w), BF16),
            jax.ShapeDtypeStruct((BATCH * N_META, w), BF16),
        ],
        scratch_shapes=[pltpu.VMEM((CONV_UROWS, w), F32), pltpu.VMEM((7, CONV_SROWS, CONV_CC), F32),
                        pltpu.VMEM((tt, w), F32), pltpu.VMEM((N_META, w), F32),
                        pltpu.VMEM((CONV_KERNEL, 8, w), F32)],
        compiler_params=_cparams(("parallel", "arbitrary")),
        name="conformer_conv",
    )(proj_x, proj_x, proj_x, proj_x, proj_x, proj_x, proj_x, proj_m, proj_m, proj_m,
      dw_w, dw_b.reshape(1, w), ln_g.reshape(1, w), ln_b.reshape(1, w))


NA_WIN = NA_KH * GRID_W


NA_ROWS_PER_TRIP = 4
NA_TBL_W = 1024


def _natten_kernel(q_ref, k_ref, v_ref, z_ref, km_ref, vm_ref, rp_ref, o_ref, bias_ref):
    lane = lax.broadcasted_iota(jnp.int32, (1, 2 * NA_HEAD_DIM), 1)
    lane_lo = lane < NA_HEAD_DIM
    scale = NA_HEAD_DIM ** -0.5
    q_sel = (jnp.where(lane_lo, scale, 0.0).astype(BF16), jnp.where(lane_lo, 0.0, scale).astype(BF16))
    km = km_ref[...]
    vm = vm_ref[...]

    qcol = lax.broadcasted_iota(jnp.int32, (GRID_W, NA_TBL_W), 0)
    kcol = lax.rem(lax.broadcasted_iota(jnp.int32, (GRID_W, NA_TBL_W), 1), GRID_W)
    wstart = jnp.clip(qcol - NA_KW // 2, 0, GRID_W - NA_KW)
    in_window = jnp.logical_and(kcol >= wstart, kcol < wstart + NA_KW)
    for e in range(2):
        rows = jnp.broadcast_to(rp_ref[e], (GRID_W, NA_TBL_W))
        for par in range(2):
            shift = (NA_TBL_W - (NA_KW - 1) - par * GRID_W) % NA_TBL_W
            t = pltpu.roll(rows, shift, 1, stride=1, stride_axis=0)
            bias_ref[par, e * GRID_W:(e + 1) * GRID_W, :] = jnp.where(in_window, t, NEG_BIG)

    def scores(r):
        rs = jnp.clip(r - NA_KH // 2, 0, GRID_ROWS - NA_KH)
        cls = rs - r + NA_KH - 1
        par = lax.rem(cls, 2)
        boff = pl.multiple_of((cls - par) * GRID_W, 2 * GRID_W)
        qs = pl.multiple_of(r * GRID_W, GRID_W)
        ks = pl.multiple_of(rs * GRID_W, GRID_W)
        q = q_ref[pl.ds(qs, GRID_W), :]
        kw = k_ref[pl.ds(ks, NA_WIN), :]
        q2 = jnp.concatenate([q * q_sel[0], q * q_sel[1]], axis=0)
        s = _dot_nt(q2, kw) + bias_ref[par, :, pl.ds(boff, NA_WIN)]
        sm = _dot_nt(q2, km)
        return qs, ks, s, sm

    def softmax(s, sm):
        m = jnp.maximum(jnp.max(s, axis=-1, keepdims=True), jnp.max(sm, axis=-1, keepdims=True))
        p = jnp.exp(s - m)
        pm = jnp.exp(sm - m)
        den = jnp.sum(p, axis=-1, keepdims=True) + jnp.sum(pm, axis=-1, keepdims=True)
        return p.astype(BF16), pm.astype(BF16), den

    def finish(qs, ks, p, pm, den):
        o2 = (_dot(p, v_ref[pl.ds(ks, NA_WIN), :]) + _dot(pm, vm)) / den
        o = jnp.where(lane_lo, o2[:GRID_W], o2[GRID_W:])
        o = o * _silu(z_ref[pl.ds(qs, GRID_W), :].astype(F32))
        o_ref[pl.ds(qs, GRID_W), :] = o.astype(o_ref.dtype)

    def trip(i, carry):
        rows = [scores(i * NA_ROWS_PER_TRIP + u) for u in range(NA_ROWS_PER_TRIP)]
        probs = [softmax(s, sm) for (_, _, s, sm) in rows]
        for (qs, ks, _, _), (p, pm, den) in zip(rows, probs):
            finish(qs, ks, p, pm, den)
        return carry

    lax.fori_loop(0, GRID_ROWS // NA_ROWS_PER_TRIP, trip, 0)


def _natten(proj_x, proj_m, rpb):
    pw = 2 * NA_HEAD_DIM
    npair = NA_HEADS // 2
    n_dr = 2 * NA_KH - 1
    rp = jnp.pad(rpb.astype(F32), ((0, 0), (0, 0), (0, GRID_W - (2 * NA_KW - 1))))
    rp = jnp.pad(rp.reshape(NA_HEADS, n_dr * GRID_W), ((0, 0), (0, NA_TBL_W - n_dr * GRID_W)))
    rp = rp.reshape(NA_HEADS, 1, NA_TBL_W)

    def xspec(col0):
        return pl.BlockSpec((SEQ, pw), lambda b, h, c=col0 // pw: (b, c + h))

    def mspec(col0):
        return pl.BlockSpec((N_META, pw), lambda b, h, c=col0 // pw: (b, c + h))

    return pl.pallas_call(
        _natten_kernel,
        grid=(BATCH, npair),
        in_specs=[
            xspec(0), xspec(NA_WIDTH), xspec(2 * NA_WIDTH), xspec(3 * NA_WIDTH),
            mspec(NA_WIDTH), mspec(2 * NA_WIDTH),
            pl.BlockSpec((2, 1, NA_TBL_W), lambda b, h: (h, 0, 0)),
        ],
        out_specs=pl.BlockSpec((SEQ, pw), lambda b, h: (b, h)),
        out_shape=jax.ShapeDtypeStruct((BATCH * SEQ, NA_WIDTH), BF16),
        scratch_shapes=[pltpu.VMEM((2, 2 * GRID_W, NA_TBL_W), F32)],
        compiler_params=_cparams(("parallel", "arbitrary")),
        name="natten",
    )(proj_x, proj_x, proj_x, proj_x, proj_m, proj_m, rp)


def kernel(x, meta_tokens, e_norm_g, e_w_in, e_conv_w, e_conv_b, e_dt_bias, e_A_log, e_D, e_ssd_norm_g, e_dw_w, e_dw_b, e_ln_g, e_ln_b, e_w_out, o_norm_g, o_w_in, o_rpb, o_w_out, final_norm_g):
    nb = x.shape[0]
    x2d = x.reshape(nb * SEQ, D_MODEL)
    meta = meta_tokens.astype(x.dtype)

    w_main, wdt = _w_prep(jnp.swapaxes(e_w_in, 1, 2)[0])

    proj_x, dt_x = _in_proj(x2d, e_norm_g[0], w_main, wdt, w_is_nk=True, name="in_proj0")
    proj_m, dt_m = _in_proj(meta, e_norm_g[0], w_main, wdt, w_is_nk=True, name="in_proj0_meta")

    def group_dt(a, rows):
        a = a[:, :2 * SSD_HEADS].reshape(-1, rows, 2, SSD_GROUPS, HEADS_PER_GROUP)
        return jnp.transpose(a, (0, 3, 2, 4, 1)).reshape(-1, SSD_GROUPS, 2 * HEADS_PER_GROUP, rows)

    dt_seq_t = jnp.concatenate([
        jnp.full((nb, SSD_GROUPS, 2 * HEADS_PER_GROUP, PAD_L), NEG_BIG, F32),
        jnp.broadcast_to(group_dt(dt_m, N_META), (nb, SSD_GROUPS, 2 * HEADS_PER_GROUP, N_META)),
        group_dt(dt_x, SEQ),
    ], axis=3)

    def group_vec(a):
        return jnp.transpose(a.reshape(2, SSD_GROUPS, HEADS_PER_GROUP), (1, 0, 2)).reshape(SSD_GROUPS, -1)

    bias_g = group_vec(e_dt_bias[0].astype(F32))
    alog_g = group_vec(e_A_log[0].astype(F32))
    dsk = jnp.repeat(e_D[0].astype(F32), SSD_HEAD_DIM).reshape(1, SSD_WIDTH)

    yg_x, yg_m, ss_x, ss_m = _ssd(
        proj_x, proj_m, dt_seq_t, e_conv_w[0], e_conv_b[0].reshape(1, -1),
        bias_g[:, :, None], alog_g[:, :, None], dsk)
    yb_x, yb_m = _conformer_conv(proj_x, proj_m, e_dw_w[0], e_dw_b[0], e_ln_g[0], e_ln_b[0])

    w_out = e_w_out[0].astype(BF16)
    wa, wb = w_out[:SSD_WIDTH], w_out[SSD_WIDTH:]
    h1_x = _out_proj0(x2d, yg_x, ss_x, e_ssd_norm_g[0], yb_x, wa, wb, name="out_proj0")
    meta_b = jnp.broadcast_to(meta[None], (nb, N_META, D_MODEL)).reshape(nb * N_META, D_MODEL)
    h1_m = _out_proj0(meta_b, yg_m, ss_m, e_ssd_norm_g[0], yb_m, wa, wb, name="out_proj0_meta")

    w_in1 = o_w_in[0].astype(BF16)
    p1_x = _in_proj(h1_x, o_norm_g[0], w_in1, name="in_proj1")
    p1_m = _in_proj(h1_m, o_norm_g[0], w_in1, name="in_proj1_meta")
    o_x = _natten(p1_x, p1_m, o_rpb[0])
    out = _out_proj1(h1_x, o_x, o_w_out[0].astype(BF16), final_norm_g, name="out_proj1")
    return out.reshape(nb, SEQ, D_MODEL)
```

```python
import functools

import jax
import jax.numpy as jnp
import numpy as np
from jax import lax
from jax.experimental import pallas as pl
from jax.experimental.pallas import tpu as pltpu

F32 = jnp.float32
BF16 = jnp.bfloat16

D_MODEL = 2048
BATCH = 4
SEQ = 2048
N_META = 16
GRID_W = 64
EPS = 1e-5

SSD_HEAD_DIM = 64
SSD_WIDTH = 2048
SSD_HEADS = 32
SSD_GROUPS = 8
SSD_STATE = 128
SSD_CONV = 5
CHUNK = 128
HEADS_PER_GROUP = SSD_HEADS // SSD_GROUPS
GROUP_W = HEADS_PER_GROUP * SSD_HEAD_DIM

CONV_WIDTH = 2048
CONV_KERNEL = 31
CONV_HALO = CONV_KERNEL // 2

NA_HEAD_DIM = 64
NA_WIDTH = 2048
NA_HEADS = 32
NA_KH = 8
NA_KW = 16
GRID_ROWS = SEQ // GRID_W

PAD_L = (-(N_META + SEQ)) % CHUNK
T_PAD = PAD_L + N_META + SEQ
N_CHUNKS = T_PAD // CHUNK
RAW_OFF = 8
SSD_CHUNKS_PER_TRIP = 2
NEG_BIG = -1e30
LOG2E = 1.4426950408889634

VMEM_LIMIT_BYTES = 56 * 1024 * 1024

COL_ZA = 0
COL_XS = 2048
COL_B = 4096
COL_C = 5120
COL_ZB = 6144
COL_GV = 8192
COL_GG = 10240
N_MAIN = 12288
DT_COLS = 128


def _cparams(sem):
    return pltpu.CompilerParams(dimension_semantics=sem, vmem_limit_bytes=VMEM_LIMIT_BYTES)


def _dot(a, b):
    return jnp.dot(a, b, preferred_element_type=F32)


def _dot_nt(a, b):
    return lax.dot_general(a, b, (((1,), (1,)), ((), ())), preferred_element_type=F32)


def _silu(x):
    return x * jax.nn.sigmoid(x)


def _softplus(x):
    return jnp.maximum(x, 0.0) + jnp.log1p(jnp.exp(-jnp.abs(x)))


def _split3(v):
    hi = v.astype(BF16)
    r1 = v - hi.astype(F32)
    mid = r1.astype(BF16)
    lo = (r1 - mid.astype(F32)).astype(BF16)
    return hi, mid, lo


W_PREP_ROWS = 1024


def _w_prep_kernel(w_ref, wdt_ref, o_ref, odt_ref):
    o_ref[...] = w_ref[...].astype(BF16)

    @pl.when(pl.program_id(0) == 0)
    def _():
        w = wdt_ref[...]
        hi = w.astype(BF16)
        lo = (w - hi.astype(F32)).astype(BF16)
        pad = jnp.zeros((DT_COLS - 2 * SSD_HEADS, w.shape[1]), BF16)
        odt_ref[...] = jnp.concatenate([hi, pad, lo, pad], axis=0)


def _w_prep(w_t):
    n_all, d = w_t.shape
    ndt = 2 * SSD_HEADS
    dt0 = COL_ZB
    tr = W_PREP_ROWS
    return pl.pallas_call(
        _w_prep_kernel,
        grid=(N_MAIN // tr,),
        in_specs=[
            pl.BlockSpec((pl.Element(tr), pl.Element(d)),
                         lambda j: ((j * (tr // ndt) + jnp.where(j >= dt0 // tr, 1, 0)) * ndt, 0)),
            pl.BlockSpec((ndt, d), lambda j: (dt0 // ndt, 0)),
        ],
        out_specs=[
            pl.BlockSpec((tr, d), lambda j: (j, 0)),
            pl.BlockSpec((2 * DT_COLS, d), lambda j: (0, 0)),
        ],
        out_shape=[
            jax.ShapeDtypeStruct((N_MAIN, d), BF16),
            jax.ShapeDtypeStruct((2 * DT_COLS, d), BF16),
        ],
        compiler_params=_cparams(("arbitrary",)),
        name="w_in_prep",
    )(w_t, w_t)


def _in_proj_kernel(x_ref, g_ref, w_ref, *rest, with_dt, w_is_nk):
    if with_dt:
        wdt_ref, o_ref, dt_ref, u_ref = rest
    else:
        o_ref, u_ref = rest

    @pl.when(pl.program_id(1) == 0)
    def _():
        x = x_ref[...]
        ms = jnp.mean(x * x, axis=-1, keepdims=True)
        y = x * lax.rsqrt(ms + EPS) * g_ref[...]
        u = y.astype(BF16)
        u_ref[...] = u
        if with_dt:
            u_lo = (y - u.astype(F32)).astype(BF16)
            w_hi = wdt_ref[:DT_COLS, :]
            w_lo = wdt_ref[DT_COLS:, :]
            dt_ref[...] = _dot_nt(u, w_hi) + _dot_nt(u_lo, w_hi) + _dot_nt(u, w_lo)

    mm = _dot_nt if w_is_nk else _dot
    o_ref[...] = mm(u_ref[...], w_ref[...]).astype(o_ref.dtype)


def _in_proj(x2d, g, w, wdt=None, *, w_is_nk=False, name):
    m, d = x2d.shape
    n = w.shape[0] if w_is_nk else w.shape[1]
    tm = min(m, 1024)
    tn = 1024
    with_dt = wdt is not None
    in_specs = [
        pl.BlockSpec((tm, d), lambda i, j: (i, 0)),
        pl.BlockSpec((1, d), lambda i, j: (0, 0)),
        pl.BlockSpec((tn, d), lambda i, j: (j, 0)) if w_is_nk else pl.BlockSpec((d, tn), lambda i, j: (0, j)),
    ]
    out_shape = [jax.ShapeDtypeStruct((m, n), BF16)]
    out_specs = [pl.BlockSpec((tm, tn), lambda i, j: (i, j))]
    args = [x2d, g.reshape(1, d), w]
    if with_dt:
        in_specs.append(pl.BlockSpec((2 * DT_COLS, d), lambda i, j: (0, 0)))
        out_shape.append(jax.ShapeDtypeStruct((m, DT_COLS), F32))
        out_specs.append(pl.BlockSpec((tm, DT_COLS), lambda i, j: (i, 0)))
        args.append(wdt)
    res = pl.pallas_call(
        functools.partial(_in_proj_kernel, with_dt=with_dt, w_is_nk=w_is_nk),
        grid=(m // tm, n // tn),
        in_specs=in_specs,
        out_specs=out_specs,
        out_shape=out_shape,
        scratch_shapes=[pltpu.VMEM((tm, d), BF16)],
        compiler_params=_cparams(("parallel", "arbitrary")),
        name=name,
    )(*args)
    return res if with_dt else res[0]


def _out_proj0_kernel(h_ref, ya_ref, ss_ref, ng_ref, yb_ref, wa_ref, wb_ref, o_ref, yan_ref):
    @pl.when(pl.program_id(1) == 0)
    def _():
        r = lax.rsqrt(ss_ref[:, 0:1] * (1.0 / SSD_WIDTH) + EPS)
        yan_ref[...] = (ya_ref[...].astype(F32) * r * ng_ref[...]).astype(BF16)

    o_ref[...] = h_ref[...] + _dot(yan_ref[...], wa_ref[...]) + _dot(yb_ref[...], wb_ref[...])


def _out_proj0(h2d, ya, ss, ng, yb, w_out, *, name):
    m, d = h2d.shape
    tm = min(m, 1024)
    tn = 512
    return pl.pallas_call(
        _out_proj0_kernel,
        grid=(m // tm, d // tn),
        in_specs=[
            pl.BlockSpec((tm, tn), lambda i, j: (i, j)),
            pl.BlockSpec((tm, SSD_WIDTH), lambda i, j: (i, 0)),
            pl.BlockSpec((tm, 128), lambda i, j: (i, 0)),
            pl.BlockSpec((1, SSD_WIDTH), lambda i, j: (0, 0)),
            pl.BlockSpec((tm, CONV_WIDTH), lambda i, j: (i, 0)),
            pl.BlockSpec((SSD_WIDTH, tn), lambda i, j: (0, j)),
            pl.BlockSpec((CONV_WIDTH, tn), lambda i, j: (SSD_WIDTH // CONV_WIDTH, j)),
        ],
        out_specs=pl.BlockSpec((tm, tn), lambda i, j: (i, j)),
        out_shape=jax.ShapeDtypeStruct((m, d), F32),
        scratch_shapes=[pltpu.VMEM((tm, SSD_WIDTH), BF16)],
        compiler_params=_cparams(("parallel", "arbitrary")),
        name=name,
    )(h2d, ya, ss, ng.reshape(1, SSD_WIDTH), yb, w_out, w_out)


def _out_proj1_kernel(h_ref, o_ref, w_ref, g_ref, out_ref):
    hn = h_ref[...] + _dot(o_ref[...], w_ref[...])
    ms = jnp.mean(hn * hn, axis=-1, keepdims=True)
    out_ref[...] = hn * lax.rsqrt(ms + EPS) * g_ref[...]


def _out_proj1(h2d, o, w, g, *, name):
    m, d = h2d.shape
    tm = 512
    return pl.pallas_call(
        _out_proj1_kernel,
        grid=(m // tm,),
        in_specs=[
            pl.BlockSpec((tm, d), lambda i: (i, 0)),
            pl.BlockSpec((tm, NA_WIDTH), lambda i: (i, 0)),
            pl.BlockSpec((NA_WIDTH, d), lambda i: (0, 0)),
            pl.BlockSpec((1, d), lambda i: (0, 0)),
        ],
        out_specs=pl.BlockSpec((tm, d), lambda i: (i, 0)),
        out_shape=jax.ShapeDtypeStruct((m, d), F32),
        compiler_params=_cparams(("parallel",)),
        name=name,
    )(h2d, o, w, g.reshape(1, d))


def _ssd_kernel(za_x, xs_x, bm_x, cm_x, za_m, xs_m, bm_m, cm_m, dtt_ref,
                cwx, cwb, cwc, cbx, cbb, cbc, biast, alogt, dsk,
                yg_x, yg_m, ss_x, ss_m,
                raw_x, raw_b, raw_c, xc, bc, cc, ybuf_f, ybuf_b, csd_t, at, cs_t, st_f, st_b):
    g = pl.program_id(1)

    for raw, m_ref, x_ref in ((raw_x, xs_m, xs_x), (raw_b, bm_m, bm_x), (raw_c, cm_m, cm_x)):
        w = raw.shape[1]
        raw[0:RAW_OFF + PAD_L, :] = jnp.zeros((RAW_OFF + PAD_L, w), F32)
        raw[RAW_OFF + PAD_L:RAW_OFF + CHUNK, :] = m_ref[...].astype(F32)
        raw[RAW_OFF + CHUNK:RAW_OFF + T_PAD, :] = x_ref[...].astype(F32)
        raw[RAW_OFF + T_PAD:RAW_OFF + T_PAD + 8, :] = jnp.zeros((8, w), F32)

    row = lax.broadcasted_iota(jnp.int32, (CHUNK, 1), 0)

    def conv_chunk(c, carry):
        base = pl.multiple_of(c * CHUNK, CHUNK)
        keep = jnp.logical_or(c > 0, row >= PAD_L)
        for raw, cw, cb, dst in ((raw_x, cwx, cbx, xc), (raw_b, cwb, cbb, bc), (raw_c, cwc, cbc, cc)):
            win = raw[pl.ds(base, CHUNK + 16), :]
            acc = cb[...]
            for k in range(SSD_CONV):
                s0 = RAW_OFF - SSD_CONV // 2 + k
                acc = acc + win[s0:s0 + CHUNK, :] * cw[k:k + 1, :]
            v = jnp.where(keep, _silu(acc), 0.0)
            dst[pl.ds(base, CHUNK), :] = v.astype(dst.dtype)
        return carry

    lax.fori_loop(0, N_CHUNKS, conv_chunk, 0)

    a_t = -jnp.exp(alogt[0])
    dtv = _softplus(dtt_ref[0, 0] + biast[0])
    at[...] = dtv * a_t
    csd_t[...] = -jnp.log2(dtv)

    li = lax.broadcasted_iota(jnp.int32, (CHUNK, CHUNK), 0)
    si = lax.broadcasted_iota(jnp.int32, (CHUNK, CHUNK), 1)
    lane = lax.broadcasted_iota(jnp.int32, (1, CHUNK), 1)
    lane_lo = lane < SSD_HEAD_DIM
    head_sel = (jnp.where(lane_lo, 1.0, 0.0).astype(BF16), jnp.where(lane_lo, 0.0, 1.0).astype(BF16))
    masks = (li >= si, li <= si)
    ends = (CHUNK - 1, 0)

    a2 = jnp.concatenate([at[:, c * CHUNK:(c + 1) * CHUNK] for c in range(N_CHUNKS)]
                         + [jnp.zeros((8, CHUNK), F32)], axis=0)
    pieces = _split3(a2)
    t_f = jnp.where(li <= si, 1.0, 0.0).astype(BF16)
    t_b = jnp.where(li >= si, 1.0, 0.0).astype(BF16)
    cs_f = sum(_dot(p, t_f) for p in pieces)
    cs_b = sum(_dot(p, t_b) for p in pieces)
    is_fwd = lax.rem(lax.broadcasted_iota(jnp.int32, (8 * (N_CHUNKS + 1), 1), 0), 8) < HEADS_PER_GROUP
    cs2 = jnp.where(is_fwd, cs_f, cs_b) * LOG2E
    for c in range(N_CHUNKS):
        cols = slice(c * CHUNK, (c + 1) * CHUNK)
        cs_t[:, cols] = cs2[8 * c:8 * c + 8]
        csd_t[:, cols] = csd_t[:, cols] + cs2[8 * c:8 * c + 8]

    st_f[...] = jnp.zeros_like(st_f)
    st_b[...] = jnp.zeros_like(st_b)

    def chunk_inputs(c):
        base = c * CHUNK if isinstance(c, int) else pl.multiple_of(c * CHUNK, CHUNK)
        xb = xc[pl.ds(base, CHUNK), :].astype(BF16)
        bb = bc[pl.ds(base, CHUNK), :]
        cb_ = cc[pl.ds(base, CHUNK), :]
        csr8 = cs_t[:, pl.ds(base, CHUNK)]
        csd8 = csd_t[:, pl.ds(base, CHUNK)]
        cbm = _dot_nt(cb_, bb)
        bt = bb.astype(F32).T
        csc8 = csr8.T
        return base, xb, cb_.astype(F32), csr8, csd8, cbm, bt, csc8

    def operands(d, inp):
        _, xb, cf, csr8, csd8, cbm, bt, csc8 = inp
        out = []
        for pair in range(2):
            xp = xb[:, pair * CHUNK:(pair + 1) * CHUNK]
            per_head = []
            decs = []
            for e in range(2):
                hd = 4 * d + 2 * pair + e
                xk = xp * head_sel[e]
                csb = jnp.broadcast_to(csc8[:, hd:hd + 1], (CHUNK, CHUNK))
                csd = csd8[hd:hd + 1, :]
                lmat = jnp.exp2(jnp.where(masks[d], csb - csd, NEG_BIG))
                mh = (cbm * lmat).astype(BF16)
                csc = (cf * jnp.exp2(csb)).astype(BF16)
                wend = csr8[hd:hd + 1, ends[d]:ends[d] + 1]
                bs = (bt * jnp.exp2(wend - csd)).astype(BF16)
                per_head.append((mh, csc, bs, xk))
                decs.append(jnp.exp2(wend))
            (mh0, csc0, bs0, xk0), (mh1, csc1, bs1, xk1) = per_head
            lhs_y = jnp.concatenate([mh0, mh1, csc0, csc1], axis=1)
            lhs_s = jnp.concatenate([bs0, bs1], axis=1)
            x_cat = jnp.concatenate([xk0, xk1], axis=0)
            out.append((lhs_y, lhs_s, x_cat, jnp.where(lane_lo, decs[0], decs[1])))
        return out

    def outputs(state, ybuf, base, ops):
        new_state = []
        for pair, (lhs_y, lhs_s, x_cat, dec) in enumerate(ops):
            ps = slice(pair * CHUNK, (pair + 1) * CHUNK)
            sp = state[:, ps]
            spb = sp.astype(BF16)
            rhs_y = jnp.concatenate([x_cat, spb * head_sel[0], spb * head_sel[1]], axis=0)
            ybuf[pl.ds(base, CHUNK), ps] = _dot(lhs_y, rhs_y)
            new_state.append(sp * dec + _dot(lhs_s, x_cat))
        return jnp.concatenate(new_state, axis=1)

    def scan_trip(chunks_f, chunks_b):
        inps_f = [chunk_inputs(c) for c in chunks_f]
        inps_b = [chunk_inputs(c) for c in chunks_b]
        ops_f = [operands(0, inp) for inp in inps_f]
        ops_b = [operands(1, inp) for inp in inps_b]
        sf = st_f[...]
        sb = st_b[...]
        for inp_f, op_f, inp_b, op_b in zip(inps_f, ops_f, inps_b, ops_b):
            sf = outputs(sf, ybuf_f, inp_f[0], op_f)
            sb = outputs(sb, ybuf_b, inp_b[0], op_b)
        st_f[...] = sf
        st_b[...] = sb

    def scan_step(i, carry):
        first = i * SSD_CHUNKS_PER_TRIP
        scan_trip([first + u for u in range(SSD_CHUNKS_PER_TRIP)],
                  [N_CHUNKS - 1 - first - u for u in range(SSD_CHUNKS_PER_TRIP)])
        return carry

    n_trips = N_CHUNKS // SSD_CHUNKS_PER_TRIP
    lax.fori_loop(0, n_trips, scan_step, 0)
    for c in range(n_trips * SSD_CHUNKS_PER_TRIP, N_CHUNKS):
        scan_trip([c], [N_CHUNKS - 1 - c])

    @pl.when(g == 0)
    def _():
        ss_x[...] = jnp.zeros_like(ss_x)
        ss_m[...] = jnp.zeros_like(ss_m)

    def finish(i, carry):
        base = pl.multiple_of(i * CHUNK, CHUNK)
        rows = pl.ds(CHUNK + base, CHUNK)
        y = ybuf_f[rows, :] + ybuf_b[rows, :] + dsk[...] * xc[rows, :]
        yg = y * _silu(za_x[pl.ds(base, CHUNK), :].astype(F32))
        yg_x[pl.ds(base, CHUNK), :] = yg.astype(yg_x.dtype)
        ss_x[pl.ds(base, CHUNK), :] = ss_x[pl.ds(base, CHUNK), :] + jnp.sum(yg * yg, axis=-1, keepdims=True)
        return carry

    lax.fori_loop(0, SEQ // CHUNK, finish, 0)

    y = ybuf_f[PAD_L:CHUNK, :] + ybuf_b[PAD_L:CHUNK, :] + dsk[...] * xc[PAD_L:CHUNK, :]
    yg = y * _silu(za_m[...].astype(F32))
    yg_m[...] = yg.astype(yg_m.dtype)
    ss_m[...] = ss_m[...] + jnp.sum(yg * yg, axis=-1, keepdims=True)


def _ssd(proj_x, proj_m, dtt, conv_w, conv_b, biast, alogt, dsk):
    gw, sn = GROUP_W, SSD_STATE
    nb = BATCH

    def xspec(width, col0):
        return pl.BlockSpec((SEQ, width), lambda b, g, c=col0 // width: (b, c + g))

    def mspec(width, col0):
        return pl.BlockSpec((N_META, width), lambda b, g, c=col0 // width: (0, c + g))

    def wspec(rows, width, col0):
        return pl.BlockSpec((rows, width), lambda b, g, c=col0 // width: (0, c + g))

    in_specs = [
        xspec(gw, COL_ZA), xspec(gw, COL_XS), xspec(sn, COL_B), xspec(sn, COL_C),
        mspec(gw, COL_ZA), mspec(gw, COL_XS), mspec(sn, COL_B), mspec(sn, COL_C),
        pl.BlockSpec((1, 1, 8, T_PAD), lambda b, g: (b, g, 0, 0)),
        wspec(SSD_CONV, gw, 0), wspec(SSD_CONV, sn, SSD_WIDTH), wspec(SSD_CONV, sn, SSD_WIDTH + SSD_GROUPS * sn),
        wspec(1, gw, 0), wspec(1, sn, SSD_WIDTH), wspec(1, sn, SSD_WIDTH + SSD_GROUPS * sn),
        pl.BlockSpec((1, 8, 1), lambda b, g: (g, 0, 0)),
        pl.BlockSpec((1, 8, 1), lambda b, g: (g, 0, 0)),
        pl.BlockSpec((1, gw), lambda b, g: (0, g)),
    ]
    out_shape = [
        jax.ShapeDtypeStruct((nb * SEQ, SSD_WIDTH), BF16),
        jax.ShapeDtypeStruct((nb * N_META, SSD_WIDTH), BF16),
        jax.ShapeDtypeStruct((nb * SEQ, 128), F32),
        jax.ShapeDtypeStruct((nb * N_META, 128), F32),
    ]
    out_specs = [
        pl.BlockSpec((SEQ, gw), lambda b, g: (b, g)),
        pl.BlockSpec((N_META, gw), lambda b, g: (b, g)),
        pl.BlockSpec((SEQ, 128), lambda b, g: (b, 0)),
        pl.BlockSpec((N_META, 128), lambda b, g: (b, 0)),
    ]
    raw_rows = RAW_OFF + T_PAD + 8
    scratch = [
        pltpu.VMEM((raw_rows, gw), F32), pltpu.VMEM((raw_rows, sn), F32), pltpu.VMEM((raw_rows, sn), F32),
        pltpu.VMEM((T_PAD, gw), F32), pltpu.VMEM((T_PAD, sn), BF16), pltpu.VMEM((T_PAD, sn), BF16),
        pltpu.VMEM((T_PAD, gw), F32), pltpu.VMEM((T_PAD, gw), F32),
        pltpu.VMEM((8, T_PAD), F32), pltpu.VMEM((8, T_PAD), F32), pltpu.VMEM((8, T_PAD), F32),
        pltpu.VMEM((sn, gw), F32), pltpu.VMEM((sn, gw), F32),
    ]
    return pl.pallas_call(
        _ssd_kernel,
        grid=(nb, SSD_GROUPS),
        in_specs=in_specs,
        out_specs=out_specs,
        out_shape=out_shape,
        scratch_shapes=scratch,
        compiler_params=_cparams(("parallel", "arbitrary")),
        name="ssd_mixer",
    )(proj_x, proj_x, proj_x, proj_x, proj_m, proj_m, proj_m, proj_m, dtt,
      conv_w, conv_w, conv_w, conv_b, conv_b, conv_b, biast, alogt, dsk)


CONV_TT = 512
CONV_RC = 32
CONV_CC = 256
CONV_LW = 256
CONV_D0 = 1
CONV_SB = 64
CONV_SROWS = 576
CONV_UROWS = CONV_SROWS + 8


def _conv_kernel(gv, gg, zb, gv_p, gg_p, gv_n, gg_n, gv_m, gg_m, zb_m, dww, dwb, lng, lnb,
                 yb_x, yb_m, ubuf, sh, cbuf, cbuf_m, wrep):
    i = pl.program_id(1)
    nt = pl.num_programs(1)
    tt = CONV_TT

    def glu(v_ref, g_ref):
        return v_ref[...].astype(F32) * jax.nn.sigmoid(g_ref[...].astype(F32))

    ubuf[0:16, :] = jnp.zeros((16, CONV_WIDTH), F32)
    ubuf[48 + tt:, :] = jnp.zeros((CONV_UROWS - 48 - tt, CONV_WIDTH), F32)
    ubuf[32:32 + tt, :] = glu(gv, gg)

    @pl.when(i == 0)
    def _():
        ubuf[16:32, :] = glu(gv_m, gg_m)

    @pl.when(i > 0)
    def _():
        ubuf[16:32, :] = glu(gv_p, gg_p)

    @pl.when(i == nt - 1)
    def _():
        ubuf[32 + tt:48 + tt, :] = jnp.zeros((16, CONV_WIDTH), F32)

    @pl.when(i < nt - 1)
    def _():
        ubuf[32 + tt:48 + tt, :] = glu(gv_n, gg_n)

    def conv_rows(row0, nrows, cs):
        outs = []
        for l0 in range(0, cs.stop - cs.start, CONV_LW):
            ls = slice(l0, l0 + CONV_LW)
            gs = slice(cs.start + l0, cs.start + l0 + CONV_LW)
            acc = jnp.broadcast_to(dwb[:, gs], (nrows, CONV_LW))
            for r in range(8):
                qs = [q for q in range(5) if 0 <= 8 * q + r - CONV_D0 < CONV_KERNEL]
                span = pl.ds(row0 + 8 * qs[0], 8 * (qs[-1] - qs[0]) + nrows)
                blk = ubuf[span, gs] if r == 0 else sh[r - 1, span, ls]
                for q in qs:
                    k = 8 * q + r - CONV_D0
                    tap = blk[8 * (q - qs[0]):8 * (q - qs[0]) + nrows, :]
                    acc = acc + tap * pltpu.repeat(wrep[k, :, gs], nrows // 8, axis=0)
            outs.append(acc)
        return jnp.concatenate(outs, axis=1)

    def norm_gate(c_rows, z_rows):
        n = c_rows.shape[-1]
        mu = jnp.sum(c_rows, axis=-1, keepdims=True) * (1.0 / n)
        var = jnp.sum(c_rows * c_rows, axis=-1, keepdims=True) * (1.0 / n) - mu * mu
        yn = (c_rows - mu) * lax.rsqrt(var + EPS) * lng[...] + lnb[...]
        return _silu(yn) * _silu(z_rows)

    for k in range(CONV_KERNEL):
        wrep[k] = jnp.broadcast_to(dww[k:k + 1, :], (8, CONV_WIDTH))

    for cc in range(CONV_WIDTH // CONV_CC):
        cs = slice(cc * CONV_CC, (cc + 1) * CONV_CC)

        def shift_block(rb, carry, cs=cs):
            base = pl.multiple_of(rb * CONV_SB, CONV_SB)
            win = ubuf[pl.ds(base, CONV_SB + 8), cs]
            for r in range(1, 8):
                sh[r - 1, pl.ds(base, CONV_SB), :] = win[r:r + CONV_SB, :]
            return carry

        lax.fori_loop(0, CONV_SROWS // CONV_SB, shift_block, 0)

        def conv_chunk(rc, carry, cs=cs):
            base = pl.multiple_of(rc * CONV_RC, CONV_RC)
            cbuf[pl.ds(base, CONV_RC), cs] = conv_rows(base + 16, CONV_RC, cs)
            return carry

        lax.fori_loop(0, tt // CONV_RC, conv_chunk, 0)

        @pl.when(i == 0)
        def _(cs=cs):
            cbuf_m[:, cs] = conv_rows(0, N_META, cs)

    def gate_chunk(rc, carry):
        for half in range(2):
            base = pl.multiple_of(rc * CONV_RC + half * (CONV_RC // 2), CONV_RC // 2)
            rows = pl.ds(base, CONV_RC // 2)
            out = norm_gate(cbuf[rows, :], zb[rows, :].astype(F32))
            yb_x[rows, :] = out.astype(yb_x.dtype)
        return carry

    lax.fori_loop(0, tt // CONV_RC, gate_chunk, 0)

    @pl.when(i == 0)
    def _():
        out = norm_gate(cbuf_m[...], zb_m[...].astype(F32))
        yb_m[...] = out.astype(yb_m.dtype)


def _conformer_conv(proj_x, proj_m, dw_w, dw_b, ln_g, ln_b):
    tt = CONV_TT
    nt = SEQ // tt
    w = CONV_WIDTH
    hb = tt // 16
    last_hb = BATCH * SEQ // 16 - 1

    def main(col0):
        return pl.BlockSpec((tt, w), lambda b, i, c=col0 // w: (b * nt + i, c))

    def prev(col0):
        return pl.BlockSpec((16, w), lambda b, i, c=col0 // w: (jnp.maximum((b * nt + i) * hb - 1, 0), c))

    def nxt(col0):
        return pl.BlockSpec((16, w), lambda b, i, c=col0 // w: (jnp.minimum((b * nt + i + 1) * hb, last_hb), c))

    def meta(col0):
        return pl.BlockSpec((N_META, w), lambda b, i, c=col0 // w: (0, c))

    vec = pl.BlockSpec((1, w), lambda b, i: (0, 0))
    return pl.pallas_call(
        _conv_kernel,
        grid=(BATCH, nt),
        in_specs=[
            main(COL_GV), main(COL_GG), main(COL_ZB),
            prev(COL_GV), prev(COL_GG), nxt(COL_GV), nxt(COL_GG),
            meta(COL_GV), meta(COL_GG), meta(COL_ZB),
            pl.BlockSpec((CONV_KERNEL, w), lambda b, i: (0, 0)), vec, vec, vec,
        ],
        out_specs=[
            pl.BlockSpec((tt, w), lambda b, i: (b * nt + i, 0)),
            pl.BlockSpec((N_META, w), lambda b, i: (b, 0)),
        ],
        out_shape=[
            jax.ShapeDtypeStruct((BATCH * SEQ, w), BF16),
            jax.ShapeDtypeStruct((BATCH * N_META, w), BF16),
        ],
        scratch_shapes=[pltpu.VMEM((CONV_UROWS, w), F32), pltpu.VMEM((7, CONV_SROWS, CONV_CC), F32),
                        pltpu.VMEM((tt, w), F32), pltpu.VMEM((N_META, w), F32),
                        pltpu.VMEM((CONV_KERNEL, 8, w), F32)],
        compiler_params=_cparams(("parallel", "arbitrary")),
        name="conformer_conv",
    )(proj_x, proj_x, proj_x, proj_x, proj_x, proj_x, proj_x, proj_m, proj_m, proj_m,
      dw_w, dw_b.reshape(1, w), ln_g.reshape(1, w), ln_b.reshape(1, w))


NA_WIN = NA_KH * GRID_W


NA_ROWS_PER_TRIP = 8
NA_TBL_W = 1024


def _natten_kernel(q_ref, k_ref, v_ref, z_ref, km_ref, vm_ref, rp_ref, o_ref, bias_ref):
    lane = lax.broadcasted_iota(jnp.int32, (1, 2 * NA_HEAD_DIM), 1)
    lane_lo = lane < NA_HEAD_DIM
    scale = NA_HEAD_DIM ** -0.5
    q_sel = (jnp.where(lane_lo, scale, 0.0).astype(BF16), jnp.where(lane_lo, 0.0, scale).astype(BF16))
    km = km_ref[...]
    vm = vm_ref[...]

    qcol = lax.broadcasted_iota(jnp.int32, (GRID_W, NA_TBL_W), 0)
    kcol = lax.rem(lax.broadcasted_iota(jnp.int32, (GRID_W, NA_TBL_W), 1), GRID_W)
    wstart = jnp.clip(qcol - NA_KW // 2, 0, GRID_W - NA_KW)
    in_window = jnp.logical_and(kcol >= wstart, kcol < wstart + NA_KW)
    for e in range(2):
        rows = jnp.broadcast_to(rp_ref[e], (GRID_W, NA_TBL_W))
        for par in range(2):
            shift = (NA_TBL_W - (NA_KW - 1) - par * GRID_W) % NA_TBL_W
            t = pltpu.roll(rows, shift, 1, stride=1, stride_axis=0)
            bias_ref[par, e * GRID_W:(e + 1) * GRID_W, :] = jnp.where(in_window, t, NEG_BIG)

    def scores(r):
        rs = jnp.clip(r - NA_KH // 2, 0, GRID_ROWS - NA_KH)
        cls = rs - r + NA_KH - 1
        par = lax.rem(cls, 2)
        boff = pl.multiple_of((cls - par) * GRID_W, 2 * GRID_W)
        qs = pl.multiple_of(r * GRID_W, GRID_W)
        ks = pl.multiple_of(rs * GRID_W, GRID_W)
        q = q_ref[pl.ds(qs, GRID_W), :]
        kw = k_ref[pl.ds(ks, NA_WIN), :]
        q2 = jnp.concatenate([q * q_sel[0], q * q_sel[1]], axis=0)
        s = _dot_nt(q2, kw) + bias_ref[par, :, pl.ds(boff, NA_WIN)]
        sm = _dot_nt(q2, km)
        return qs, ks, s, sm

    def softmax(s, sm):
        m = jnp.maximum(jnp.max(s, axis=-1, keepdims=True), jnp.max(sm, axis=-1, keepdims=True))
        p = jnp.exp(s - m)
        pm = jnp.exp(sm - m)
        den = jnp.sum(p, axis=-1, keepdims=True) + jnp.sum(pm, axis=-1, keepdims=True)
        return p.astype(BF16), pm.astype(BF16), den

    def finish(qs, ks, p, pm, den):
        o2 = (_dot(p, v_ref[pl.ds(ks, NA_WIN), :]) + _dot(pm, vm)) / den
        o = jnp.where(lane_lo, o2[:GRID_W], o2[GRID_W:])
        o = o * _silu(z_ref[pl.ds(qs, GRID_W), :].astype(F32))
        o_ref[pl.ds(qs, GRID_W), :] = o.astype(o_ref.dtype)

    def trip(i, carry):
        rows = [scores(i * NA_ROWS_PER_TRIP + u) for u in range(NA_ROWS_PER_TRIP)]
        probs = [softmax(s, sm) for (_, _, s, sm) in rows]
        for (qs, ks, _, _), (p, pm, den) in zip(rows, probs):
            finish(qs, ks, p, pm, den)
        return carry

    lax.fori_loop(0, GRID_ROWS // NA_ROWS_PER_TRIP, trip, 0)


def _natten(proj_x, proj_m, rpb):
    pw = 2 * NA_HEAD_DIM
    npair = NA_HEADS // 2
    n_dr = 2 * NA_KH - 1
    rp = jnp.pad(rpb.astype(F32), ((0, 0), (0, 0), (0, GRID_W - (2 * NA_KW - 1))))
    rp = jnp.pad(rp.reshape(NA_HEADS, n_dr * GRID_W), ((0, 0), (0, NA_TBL_W - n_dr * GRID_W)))
    rp = rp.reshape(NA_HEADS, 1, NA_TBL_W)

    def xspec(col0):
        return pl.BlockSpec((SEQ, pw), lambda b, h, c=col0 // pw: (b, c + h))

    def mspec(col0):
        return pl.BlockSpec((N_META, pw), lambda b, h, c=col0 // pw: (b, c + h))

    return pl.pallas_call(
        _natten_kernel,
        grid=(BATCH, npair),
        in_specs=[
            xspec(0), xspec(NA_WIDTH), xspec(2 * NA_WIDTH), xspec(3 * NA_WIDTH),
            mspec(NA_WIDTH), mspec(2 * NA_WIDTH),
            pl.BlockSpec((2, 1, NA_TBL_W), lambda b, h: (h, 0, 0)),
        ],
        out_specs=pl.BlockSpec((SEQ, pw), lambda b, h: (b, h)),
        out_shape=jax.ShapeDtypeStruct((BATCH * SEQ, NA_WIDTH), BF16),
        scratch_shapes=[pltpu.VMEM((2, 2 * GRID_W, NA_TBL_W), F32)],
        compiler_params=_cparams(("parallel", "arbitrary")),
        name="natten",
    )(proj_x, proj_x, proj_x, proj_x, proj_m, proj_m, rp)


def kernel(x, meta_tokens, e_norm_g, e_w_in, e_conv_w, e_conv_b, e_dt_bias, e_A_log, e_D, e_ssd_norm_g, e_dw_w, e_dw_b, e_ln_g, e_ln_b, e_w_out, o_norm_g, o_w_in, o_rpb, o_w_out, final_norm_g):
    nb = x.shape[0]
    x2d = x.reshape(nb * SEQ, D_MODEL)
    meta = meta_tokens.astype(x.dtype)

    w_main, wdt = _w_prep(jnp.swapaxes(e_w_in, 1, 2)[0])

    proj_x, dt_x = _in_proj(x2d, e_norm_g[0], w_main, wdt, w_is_nk=True, name="in_proj0")
    proj_m, dt_m = _in_proj(meta, e_norm_g[0], w_main, wdt, w_is_nk=True, name="in_proj0_meta")

    def group_dt(a, rows):
        a = a[:, :2 * SSD_HEADS].reshape(-1, rows, 2, SSD_GROUPS, HEADS_PER_GROUP)
        return jnp.transpose(a, (0, 3, 2, 4, 1)).reshape(-1, SSD_GROUPS, 2 * HEADS_PER_GROUP, rows)

    dt_seq_t = jnp.concatenate([
        jnp.full((nb, SSD_GROUPS, 2 * HEADS_PER_GROUP, PAD_L), NEG_BIG, F32),
        jnp.broadcast_to(group_dt(dt_m, N_META), (nb, SSD_GROUPS, 2 * HEADS_PER_GROUP, N_META)),
        group_dt(dt_x, SEQ),
    ], axis=3)

    def group_vec(a):
        return jnp.transpose(a.reshape(2, SSD_GROUPS, HEADS_PER_GROUP), (1, 0, 2)).reshape(SSD_GROUPS, -1)

    bias_g = group_vec(e_dt_bias[0].astype(F32))
    alog_g = group_vec(e_A_log[0].astype(F32))
    dsk = jnp.repeat(e_D[0].astype(F32), SSD_HEAD_DIM).reshape(1, SSD_WIDTH)

    yg_x, yg_m, ss_x, ss_m = _ssd(
        proj_x, proj_m, dt_seq_t, e_conv_w[0], e_conv_b[0].reshape(1, -1),
        bias_g[:, :, None], alog_g[:, :, None], dsk)
    yb_x, yb_m = _conformer_conv(proj_x, proj_m, e_dw_w[0], e_dw_b[0], e_ln_g[0], e_ln_b[0])

    w_out = e_w_out[0].astype(BF16)
    h1_x = _out_proj0(x2d, yg_x, ss_x, e_ssd_norm_g[0], yb_x, w_out, name="out_proj0")
    meta_b = jnp.broadcast_to(meta[None], (nb, N_META, D_MODEL)).reshape(nb * N_META, D_MODEL)
    h1_m = _out_proj0(meta_b, yg_m, ss_m, e_ssd_norm_g[0], yb_m, w_out, name="out_proj0_meta")

    w_in1 = o_w_in[0].astype(BF16)
    p1_x = _in_proj(h1_x, o_norm_g[0], w_in1, name="in_proj1")
    p1_m = _in_proj(h1_m, o_norm_g[0], w_in1, name="in_proj1_meta")
    o_x = _natten(p1_x, p1_m, o_rpb[0])
    out = _out_proj1(h1_x, o_x, o_w_out[0].astype(BF16), final_norm_g, name="out_proj1")
    return out.reshape(nb, SEQ, D_MODEL)
```

```python
import functools

import jax
import jax.numpy as jnp
import numpy as np
from jax import lax
from jax.experimental import pallas as pl
from jax.experimental.pallas import tpu as pltpu

F32 = jnp.float32
BF16 = jnp.bfloat16

D_MODEL = 2048
BATCH = 4
SEQ = 2048
N_META = 16
GRID_W = 64
EPS = 1e-5

SSD_HEAD_DIM = 64
SSD_WIDTH = 2048
SSD_HEADS = 32
SSD_GROUPS = 8
SSD_STATE = 128
SSD_CONV = 5
CHUNK = 128
HEADS_PER_GROUP = SSD_HEADS // SSD_GROUPS
GROUP_W = HEADS_PER_GROUP * SSD_HEAD_DIM

CONV_WIDTH = 2048
CONV_KERNEL = 31
CONV_HALO = CONV_KERNEL // 2

NA_HEAD_DIM = 64
NA_WIDTH = 2048
NA_HEADS = 32
NA_KH = 8
NA_KW = 16
GRID_ROWS = SEQ // GRID_W

PAD_L = (-(N_META + SEQ)) % CHUNK
T_PAD = PAD_L + N_META + SEQ
N_CHUNKS = T_PAD // CHUNK
RAW_OFF = 8
SSD_CHUNKS_PER_TRIP = 2
NEG_BIG = -1e30
LOG2E = 1.4426950408889634

VMEM_LIMIT_BYTES = 56 * 1024 * 1024

COL_ZA = 0
COL_XS = 2048
COL_B = 4096
COL_C = 5120
COL_ZB = 6144
COL_GV = 8192
COL_GG = 10240
N_MAIN = 12288
DT_COLS = 128


def _cparams(sem):
    return pltpu.CompilerParams(dimension_semantics=sem, vmem_limit_bytes=VMEM_LIMIT_BYTES)


def _dot(a, b):
    return jnp.dot(a, b, preferred_element_type=F32)


def _dot_nt(a, b):
    return lax.dot_general(a, b, (((1,), (1,)), ((), ())), preferred_element_type=F32)


def _silu(x):
    return x * jax.nn.sigmoid(x)


def _softplus(x):
    return jnp.maximum(x, 0.0) + jnp.log1p(jnp.exp(-jnp.abs(x)))


def _split3(v):
    hi = v.astype(BF16)
    r1 = v - hi.astype(F32)
    mid = r1.astype(BF16)
    lo = (r1 - mid.astype(F32)).astype(BF16)
    return hi, mid, lo


W_PREP_ROWS = 1024


def _w_prep_kernel(w_ref, wdt_ref, o_ref, odt_ref):
    o_ref[...] = w_ref[...].astype(BF16)

    @pl.when(pl.program_id(0) == 0)
    def _():
        w = wdt_ref[...]
        hi = w.astype(BF16)
        lo = (w - hi.astype(F32)).astype(BF16)
        pad = jnp.zeros((DT_COLS - 2 * SSD_HEADS, w.shape[1]), BF16)
        odt_ref[...] = jnp.concatenate([hi, pad, lo, pad], axis=0)


def _w_prep(w_t):
    n_all, d = w_t.shape
    ndt = 2 * SSD_HEADS
    dt0 = COL_ZB
    tr = W_PREP_ROWS
    return pl.pallas_call(
        _w_prep_kernel,
        grid=(N_MAIN // tr,),
        in_specs=[
            pl.BlockSpec((pl.Element(tr), pl.Element(d)),
                         lambda j: ((j * (tr // ndt) + jnp.where(j >= dt0 // tr, 1, 0)) * ndt, 0)),
            pl.BlockSpec((ndt, d), lambda j: (dt0 // ndt, 0)),
        ],
        out_specs=[
            pl.BlockSpec((tr, d), lambda j: (j, 0)),
            pl.BlockSpec((2 * DT_COLS, d), lambda j: (0, 0)),
        ],
        out_shape=[
            jax.ShapeDtypeStruct((N_MAIN, d), BF16),
            jax.ShapeDtypeStruct((2 * DT_COLS, d), BF16),
        ],
        compiler_params=_cparams(("arbitrary",)),
        name="w_in_prep",
    )(w_t, w_t)


def _in_proj_kernel(x_ref, g_ref, w_ref, *rest, with_dt, w_is_nk):
    if with_dt:
        wdt_ref, o_ref, dt_ref, u_ref = rest
    else:
        o_ref, u_ref = rest

    @pl.when(pl.program_id(1) == 0)
    def _():
        x = x_ref[...]
        ms = jnp.mean(x * x, axis=-1, keepdims=True)
        y = x * lax.rsqrt(ms + EPS) * g_ref[...]
        u = y.astype(BF16)
        u_ref[...] = u
        if with_dt:
            u_lo = (y - u.astype(F32)).astype(BF16)
            w_hi = wdt_ref[:DT_COLS, :]
            w_lo = wdt_ref[DT_COLS:, :]
            dt_ref[...] = _dot_nt(u, w_hi) + _dot_nt(u_lo, w_hi) + _dot_nt(u, w_lo)

    mm = _dot_nt if w_is_nk else _dot
    o_ref[...] = mm(u_ref[...], w_ref[...]).astype(o_ref.dtype)


def _in_proj(x2d, g, w, wdt=None, *, w_is_nk=False, name):
    m, d = x2d.shape
    n = w.shape[0] if w_is_nk else w.shape[1]
    tm = min(m, 1024)
    tn = 1024
    with_dt = wdt is not None
    in_specs = [
        pl.BlockSpec((tm, d), lambda i, j: (i, 0)),
        pl.BlockSpec((1, d), lambda i, j: (0, 0)),
        pl.BlockSpec((tn, d), lambda i, j: (j, 0)) if w_is_nk else pl.BlockSpec((d, tn), lambda i, j: (0, j)),
    ]
    out_shape = [jax.ShapeDtypeStruct((m, n), BF16)]
    out_specs = [pl.BlockSpec((tm, tn), lambda i, j: (i, j))]
    args = [x2d, g.reshape(1, d), w]
    if with_dt:
        in_specs.append(pl.BlockSpec((2 * DT_COLS, d), lambda i, j: (0, 0)))
        out_shape.append(jax.ShapeDtypeStruct((m, DT_COLS), F32))
        out_specs.append(pl.BlockSpec((tm, DT_COLS), lambda i, j: (i, 0)))
        args.append(wdt)
    res = pl.pallas_call(
        functools.partial(_in_proj_kernel, with_dt=with_dt, w_is_nk=w_is_nk),
        grid=(m // tm, n // tn),
        in_specs=in_specs,
        out_specs=out_specs,
        out_shape=out_shape,
        scratch_shapes=[pltpu.VMEM((tm, d), BF16)],
        compiler_params=_cparams(("parallel", "arbitrary")),
        name=name,
    )(*args)
    return res if with_dt else res[0]


def _out_proj0_kernel(h_ref, ya_ref, ss_ref, ng_ref, yb_ref, wa_ref, wb_ref, o_ref, yan_ref):
    @pl.when(pl.program_id(1) == 0)
    def _():
        r = lax.rsqrt(ss_ref[:, 0:1] * (1.0 / SSD_WIDTH) + EPS)
        yan_ref[...] = (ya_ref[...].astype(F32) * r * ng_ref[...]).astype(BF16)

    o_ref[...] = h_ref[...] + _dot(yan_ref[...], wa_ref[...]) + _dot(yb_ref[...], wb_ref[...])


def _out_proj0(h2d, ya, ss, ng, yb, w_out, *, name):
    m, d = h2d.shape
    tm = min(m, 1024)
    tn = 512
    return pl.pallas_call(
        _out_proj0_kernel,
        grid=(m // tm, d // tn),
        in_specs=[
            pl.BlockSpec((tm, tn), lambda i, j: (i, j)),
            pl.BlockSpec((tm, SSD_WIDTH), lambda i, j: (i, 0)),
            pl.BlockSpec((tm, 128), lambda i, j: (i, 0)),
            pl.BlockSpec((1, SSD_WIDTH), lambda i, j: (0, 0)),
            pl.BlockSpec((tm, CONV_WIDTH), lambda i, j: (i, 0)),
            pl.BlockSpec((SSD_WIDTH, tn), lambda i, j: (0, j)),
            pl.BlockSpec((CONV_WIDTH, tn), lambda i, j: (SSD_WIDTH // CONV_WIDTH, j)),
        ],
        out_specs=pl.BlockSpec((tm, tn), lambda i, j: (i, j)),
        out_shape=jax.ShapeDtypeStruct((m, d), F32),
        scratch_shapes=[pltpu.VMEM((tm, SSD_WIDTH), BF16)],
        compiler_params=_cparams(("parallel", "arbitrary")),
        name=name,
    )(h2d, ya, ss, ng.reshape(1, SSD_WIDTH), yb, w_out, w_out)


def _out_proj1_kernel(h_ref, o_ref, w_ref, g_ref, out_ref):
    hn = h_ref[...] + _dot(o_ref[...], w_ref[...])
    ms = jnp.mean(hn * hn, axis=-1, keepdims=True)
    out_ref[...] = hn * lax.rsqrt(ms + EPS) * g_ref[...]


def _out_proj1(h2d, o, w, g, *, name):
    m, d = h2d.shape
    tm = 512
    return pl.pallas_call(
        _out_proj1_kernel,
        grid=(m // tm,),
        in_specs=[
            pl.BlockSpec((tm, d), lambda i: (i, 0)),
            pl.BlockSpec((tm, NA_WIDTH), lambda i: (i, 0)),
            pl.BlockSpec((NA_WIDTH, d), lambda i: (0, 0)),
            pl.BlockSpec((1, d), lambda i: (0, 0)),
        ],
        out_specs=pl.BlockSpec((tm, d), lambda i: (i, 0)),
        out_shape=jax.ShapeDtypeStruct((m, d), F32),
        compiler_params=_cparams(("parallel",)),
        name=name,
    )(h2d, o, w, g.reshape(1, d))


def _ssd_kernel(za_x, xs_x, bm_x, cm_x, za_m, xs_m, bm_m, cm_m, dtt_ref,
                cwx, cwb, cwc, cbx, cbb, cbc, biast, alogt, dsk,
                yg_x, yg_m, ss_x, ss_m,
                raw_x, raw_b, raw_c, xc, bc, cc, ybuf_f, ybuf_b, csd_t, at, cs_t, st_f, st_b):
    g = pl.program_id(1)

    for raw, m_ref, x_ref in ((raw_x, xs_m, xs_x), (raw_b, bm_m, bm_x), (raw_c, cm_m, cm_x)):
        w = raw.shape[1]
        raw[0:RAW_OFF + PAD_L, :] = jnp.zeros((RAW_OFF + PAD_L, w), F32)
        raw[RAW_OFF + PAD_L:RAW_OFF + CHUNK, :] = m_ref[...].astype(F32)
        raw[RAW_OFF + CHUNK:RAW_OFF + T_PAD, :] = x_ref[...].astype(F32)
        raw[RAW_OFF + T_PAD:RAW_OFF + T_PAD + 8, :] = jnp.zeros((8, w), F32)

    row = lax.broadcasted_iota(jnp.int32, (CHUNK, 1), 0)

    def conv_chunk(c, carry):
        base = pl.multiple_of(c * CHUNK, CHUNK)
        keep = jnp.logical_or(c > 0, row >= PAD_L)
        for raw, cw, cb, dst in ((raw_x, cwx, cbx, xc), (raw_b, cwb, cbb, bc), (raw_c, cwc, cbc, cc)):
            win = raw[pl.ds(base, CHUNK + 16), :]
            acc = cb[...]
            for k in range(SSD_CONV):
                s0 = RAW_OFF - SSD_CONV // 2 + k
                acc = acc + win[s0:s0 + CHUNK, :] * cw[k:k + 1, :]
            v = jnp.where(keep, _silu(acc), 0.0)
            dst[pl.ds(base, CHUNK), :] = v.astype(dst.dtype)
        return carry

    lax.fori_loop(0, N_CHUNKS, conv_chunk, 0)

    a_t = -jnp.exp(alogt[0])
    dtv = _softplus(dtt_ref[0, 0] + biast[0])
    at[...] = dtv * a_t
    csd_t[...] = -jnp.log2(dtv)

    li = lax.broadcasted_iota(jnp.int32, (CHUNK, CHUNK), 0)
    si = lax.broadcasted_iota(jnp.int32, (CHUNK, CHUNK), 1)
    lane = lax.broadcasted_iota(jnp.int32, (1, CHUNK), 1)
    lane_lo = lane < SSD_HEAD_DIM
    head_sel = (jnp.where(lane_lo, 1.0, 0.0).astype(BF16), jnp.where(lane_lo, 0.0, 1.0).astype(BF16))
    masks = (li >= si, li <= si)
    ends = (CHUNK - 1, 0)

    a2 = jnp.concatenate([at[:, c * CHUNK:(c + 1) * CHUNK] for c in range(N_CHUNKS)]
                         + [jnp.zeros((8, CHUNK), F32)], axis=0)
    pieces = _split3(a2)
    t_f = jnp.where(li <= si, 1.0, 0.0).astype(BF16)
    t_b = jnp.where(li >= si, 1.0, 0.0).astype(BF16)
    cs_f = sum(_dot(p, t_f) for p in pieces)
    cs_b = sum(_dot(p, t_b) for p in pieces)
    is_fwd = lax.rem(lax.broadcasted_iota(jnp.int32, (8 * (N_CHUNKS + 1), 1), 0), 8) < HEADS_PER_GROUP
    cs2 = jnp.where(is_fwd, cs_f, cs_b) * LOG2E
    for c in range(N_CHUNKS):
        cols = slice(c * CHUNK, (c + 1) * CHUNK)
        cs_t[:, cols] = cs2[8 * c:8 * c + 8]
        csd_t[:, cols] = csd_t[:, cols] + cs2[8 * c:8 * c + 8]

    st_f[...] = jnp.zeros_like(st_f)
    st_b[...] = jnp.zeros_like(st_b)

    def chunk_inputs(c):
        base = c * CHUNK if isinstance(c, int) else pl.multiple_of(c * CHUNK, CHUNK)
        xb = xc[pl.ds(base, CHUNK), :].astype(BF16)
        bb = bc[pl.ds(base, CHUNK), :]
        cb_ = cc[pl.ds(base, CHUNK), :]
        csr8 = cs_t[:, pl.ds(base, CHUNK)]
        csd8 = csd_t[:, pl.ds(base, CHUNK)]
        cbm = _dot_nt(cb_, bb)
        bt = bb.astype(F32).T
        csc8 = csr8.T
        return base, xb, cb_.astype(F32), csr8, csd8, cbm, bt, csc8

    def operands(d, inp):
        _, xb, cf, csr8, csd8, cbm, bt, csc8 = inp
        out = []
        for pair in range(2):
            xp = xb[:, pair * CHUNK:(pair + 1) * CHUNK]
            per_head = []
            decs = []
            for e in range(2):
                hd = 4 * d + 2 * pair + e
                xk = xp * head_sel[e]
                csb = jnp.broadcast_to(csc8[:, hd:hd + 1], (CHUNK, CHUNK))
                csd = csd8[hd:hd + 1, :]
                lmat = jnp.exp2(jnp.where(masks[d], csb - csd, NEG_BIG))
                mh = (cbm * lmat).astype(BF16)
                csc = (cf * jnp.exp2(csb)).astype(BF16)
                wend = csr8[hd:hd + 1, ends[d]:ends[d] + 1]
                bs = (bt * jnp.exp2(wend - csd)).astype(BF16)
                per_head.append((mh, csc, bs, xk))
                decs.append(jnp.exp2(wend))
            (mh0, csc0, bs0, xk0), (mh1, csc1, bs1, xk1) = per_head
            lhs_y = jnp.concatenate([mh0, mh1, csc0, csc1], axis=1)
            lhs_s = jnp.concatenate([bs0, bs1], axis=1)
            x_cat = jnp.concatenate([xk0, xk1], axis=0)
            out.append((lhs_y, lhs_s, x_cat, jnp.where(lane_lo, decs[0], decs[1])))
        return out

    def outputs(state, ybuf, base, ops):
        new_state = []
        for pair, (lhs_y, lhs_s, x_cat, dec) in enumerate(ops):
            ps = slice(pair * CHUNK, (pair + 1) * CHUNK)
            sp = state[:, ps]
            spb = sp.astype(BF16)
            rhs_y = jnp.concatenate([x_cat, spb * head_sel[0], spb * head_sel[1]], axis=0)
            ybuf[pl.ds(base, CHUNK), ps] = _dot(lhs_y, rhs_y)
            new_state.append(sp * dec + _dot(lhs_s, x_cat))
        return jnp.concatenate(new_state, axis=1)

    def scan_trip(chunks_f, chunks_b):
        inps_f = [chunk_inputs(c) for c in chunks_f]
        inps_b = [chunk_inputs(c) for c in chunks_b]
        ops_f = [operands(0, inp) for inp in inps_f]
        ops_b = [operands(1, inp) for inp in inps_b]
        sf = st_f[...]
        sb = st_b[...]
        for inp_f, op_f, inp_b, op_b in zip(inps_f, ops_f, inps_b, ops_b):
            sf = outputs(sf, ybuf_f, inp_f[0], op_f)
            sb = outputs(sb, ybuf_b, inp_b[0], op_b)
        st_f[...] = sf
        st_b[...] = sb

    def scan_step(i, carry):
        first = i * SSD_CHUNKS_PER_TRIP
        scan_trip([first + u for u in range(SSD_CHUNKS_PER_TRIP)],
                  [N_CHUNKS - 1 - first - u for u in range(SSD_CHUNKS_PER_TRIP)])
        return carry

    n_trips = N_CHUNKS // SSD_CHUNKS_PER_TRIP
    lax.fori_loop(0, n_trips, scan_step, 0)
    for c in range(n_trips * SSD_CHUNKS_PER_TRIP, N_CHUNKS):
        scan_trip([c], [N_CHUNKS - 1 - c])

    @pl.when(g == 0)
    def _():
        ss_x[...] = jnp.zeros_like(ss_x)
        ss_m[...] = jnp.zeros_like(ss_m)

    def finish(i, carry):
        for u in range(2):
            base = pl.multiple_of((2 * i + u) * CHUNK, CHUNK)
            rows = pl.ds(CHUNK + base, CHUNK)
            y = ybuf_f[rows, :] + ybuf_b[rows, :] + dsk[...] * xc[rows, :]
            yg = y * _silu(za_x[pl.ds(base, CHUNK), :].astype(F32))
            yg_x[pl.ds(base, CHUNK), :] = yg.astype(yg_x.dtype)
            ss_x[pl.ds(base, CHUNK), :] = ss_x[pl.ds(base, CHUNK), :] + jnp.sum(yg * yg, axis=-1, keepdims=True)
        return carry

    lax.fori_loop(0, SEQ // (2 * CHUNK), finish, 0)

    y = ybuf_f[PAD_L:CHUNK, :] + ybuf_b[PAD_L:CHUNK, :] + dsk[...] * xc[PAD_L:CHUNK, :]
    yg = y * _silu(za_m[...].astype(F32))
    yg_m[...] = yg.astype(yg_m.dtype)
    ss_m[...] = ss_m[...] + jnp.sum(yg * yg, axis=-1, keepdims=True)


def _ssd(proj_x, proj_m, dtt, conv_w, conv_b, biast, alogt, dsk):
    gw, sn = GROUP_W, SSD_STATE
    nb = BATCH

    def xspec(width, col0):
        return pl.BlockSpec((SEQ, width), lambda b, g, c=col0 // width: (b, c + g))

    def mspec(width, col0):
        return pl.BlockSpec((N_META, width), lambda b, g, c=col0 // width: (0, c + g))

    def wspec(rows, width, col0):
        return pl.BlockSpec((rows, width), lambda b, g, c=col0 // width: (0, c + g))

    in_specs = [
        xspec(gw, COL_ZA), xspec(gw, COL_XS), xspec(sn, COL_B), xspec(sn, COL_C),
        mspec(gw, COL_ZA), mspec(gw, COL_XS), mspec(sn, COL_B), mspec(sn, COL_C),
        pl.BlockSpec((1, 1, 8, T_PAD), lambda b, g: (b, g, 0, 0)),
        wspec(SSD_CONV, gw, 0), wspec(SSD_CONV, sn, SSD_WIDTH), wspec(SSD_CONV, sn, SSD_WIDTH + SSD_GROUPS * sn),
        wspec(1, gw, 0), wspec(1, sn, SSD_WIDTH), wspec(1, sn, SSD_WIDTH + SSD_GROUPS * sn),
        pl.BlockSpec((1, 8, 1), lambda b, g: (g, 0, 0)),
        pl.BlockSpec((1, 8, 1), lambda b, g: (g, 0, 0)),
        pl.BlockSpec((1, gw), lambda b, g: (0, g)),
    ]
    out_shape = [
        jax.ShapeDtypeStruct((nb * SEQ, SSD_WIDTH), BF16),
        jax.ShapeDtypeStruct((nb * N_META, SSD_WIDTH), BF16),
        jax.ShapeDtypeStruct((nb * SEQ, 128), F32),
        jax.ShapeDtypeStruct((nb * N_META, 128), F32),
    ]
    out_specs = [
        pl.BlockSpec((SEQ, gw), lambda b, g: (b, g)),
        pl.BlockSpec((N_META, gw), lambda b, g: (b, g)),
        pl.BlockSpec((SEQ, 128), lambda b, g: (b, 0)),
        pl.BlockSpec((N_META, 128), lambda b, g: (b, 0)),
    ]
    raw_rows = RAW_OFF + T_PAD + 8
    scratch = [
        pltpu.VMEM((raw_rows, gw), F32), pltpu.VMEM((raw_rows, sn), F32), pltpu.VMEM((raw_rows, sn), F32),
        pltpu.VMEM((T_PAD, gw), F32), pltpu.VMEM((T_PAD, sn), BF16), pltpu.VMEM((T_PAD, sn), BF16),
        pltpu.VMEM((T_PAD, gw), F32), pltpu.VMEM((T_PAD, gw), F32),
        pltpu.VMEM((8, T_PAD), F32), pltpu.VMEM((8, T_PAD), F32), pltpu.VMEM((8, T_PAD), F32),
        pltpu.VMEM((sn, gw), F32), pltpu.VMEM((sn, gw), F32),
    ]
    return pl.pallas_call(
        _ssd_kernel,
        grid=(nb, SSD_GROUPS),
        in_specs=in_specs,
        out_specs=out_specs,
        out_shape=out_shape,
        scratch_shapes=scratch,
        compiler_params=_cparams(("parallel", "arbitrary")),
        name="ssd_mixer",
    )(proj_x, proj_x, proj_x, proj_x, proj_m, proj_m, proj_m, proj_m, dtt,
      conv_w, conv_w, conv_w, conv_b, conv_b, conv_b, biast, alogt, dsk)


CONV_TT = 512
CONV_RC = 64
CONV_GR = 16
CONV_CC = 128
CONV_LW = 128
CONV_D0 = 1
CONV_SB = 64
CONV_SROWS = 576
CONV_UROWS = CONV_SROWS + 8


def _conv_kernel(gv, gg, zb, gv_p, gg_p, gv_n, gg_n, gv_m, gg_m, zb_m, dww, dwb, lng, lnb,
                 yb_x, yb_m, ubuf, sh, cbuf, cbuf_m, wrep):
    i = pl.program_id(1)
    nt = pl.num_programs(1)
    tt = CONV_TT

    def glu(v_ref, g_ref):
        return v_ref[...].astype(F32) * jax.nn.sigmoid(g_ref[...].astype(F32))

    ubuf[0:16, :] = jnp.zeros((16, CONV_WIDTH), F32)
    ubuf[48 + tt:, :] = jnp.zeros((CONV_UROWS - 48 - tt, CONV_WIDTH), F32)
    ubuf[32:32 + tt, :] = glu(gv, gg)

    @pl.when(i == 0)
    def _():
        ubuf[16:32, :] = glu(gv_m, gg_m)

    @pl.when(i > 0)
    def _():
        ubuf[16:32, :] = glu(gv_p, gg_p)

    @pl.when(i == nt - 1)
    def _():
        ubuf[32 + tt:48 + tt, :] = jnp.zeros((16, CONV_WIDTH), F32)

    @pl.when(i < nt - 1)
    def _():
        ubuf[32 + tt:48 + tt, :] = glu(gv_n, gg_n)

    def conv_rows(row0, nrows, cs):
        outs = []
        for l0 in range(0, cs.stop - cs.start, CONV_LW):
            ls = slice(l0, l0 + CONV_LW)
            gs = slice(cs.start + l0, cs.start + l0 + CONV_LW)
            acc = jnp.broadcast_to(dwb[:, gs], (nrows, CONV_LW))
            for r in range(8):
                qs = [q for q in range(5) if 0 <= 8 * q + r - CONV_D0 < CONV_KERNEL]
                span = pl.ds(row0 + 8 * qs[0], 8 * (qs[-1] - qs[0]) + nrows)
                blk = ubuf[span, gs] if r == 0 else sh[r - 1, span, ls]
                for q in qs:
                    k = 8 * q + r - CONV_D0
                    tap = blk[8 * (q - qs[0]):8 * (q - qs[0]) + nrows, :]
                    acc = acc + tap * pltpu.repeat(wrep[k, :, gs], nrows // 8, axis=0)
            outs.append(acc)
        return jnp.concatenate(outs, axis=1)

    def norm_gate(c_rows, z_rows):
        n = c_rows.shape[-1]
        mu = jnp.sum(c_rows, axis=-1, keepdims=True) * (1.0 / n)
        var = jnp.sum(c_rows * c_rows, axis=-1, keepdims=True) * (1.0 / n) - mu * mu
        yn = (c_rows - mu) * lax.rsqrt(var + EPS) * lng[...] + lnb[...]
        return _silu(yn) * _silu(z_rows)

    for k in range(CONV_KERNEL):
        wrep[k] = jnp.broadcast_to(dww[k:k + 1, :], (8, CONV_WIDTH))

    for cc in range(CONV_WIDTH // CONV_CC):
        cs = slice(cc * CONV_CC, (cc + 1) * CONV_CC)

        def shift_block(rb, carry, cs=cs):
            base = pl.multiple_of(rb * CONV_SB, CONV_SB)
            win = ubuf[pl.ds(base, CONV_SB + 8), cs]
            for r in range(1, 8):
                sh[r - 1, pl.ds(base, CONV_SB), :] = win[r:r + CONV_SB, :]
            return carry

        lax.fori_loop(0, CONV_SROWS // CONV_SB, shift_block, 0)

        def conv_chunk(rc, carry, cs=cs):
            base = pl.multiple_of(rc * CONV_RC, CONV_RC)
            cbuf[pl.ds(base, CONV_RC), cs] = conv_rows(base + 16, CONV_RC, cs)
            return carry

        lax.fori_loop(0, tt // CONV_RC, conv_chunk, 0)

        @pl.when(i == 0)
        def _(cs=cs):
            cbuf_m[:, cs] = conv_rows(0, N_META, cs)

    def gate_chunk(rc, carry):
        for half in range(2):
            base = pl.multiple_of(rc * (2 * CONV_GR) + half * CONV_GR, CONV_GR)
            rows = pl.ds(base, CONV_GR)
            out = norm_gate(cbuf[rows, :], zb[rows, :].astype(F32))
            yb_x[rows, :] = out.astype(yb_x.dtype)
        return carry

    lax.fori_loop(0, tt // (2 * CONV_GR), gate_chunk, 0)

    @pl.when(i == 0)
    def _():
        out = norm_gate(cbuf_m[...], zb_m[...].astype(F32))
        yb_m[...] = out.astype(yb_m.dtype)


def _conformer_conv(proj_x, proj_m, dw_w, dw_b, ln_g, ln_b):
    tt = CONV_TT
    nt = SEQ // tt
    w = CONV_WIDTH
    hb = tt // 16
    last_hb = BATCH * SEQ // 16 - 1

    def main(col0):
        return pl.BlockSpec((tt, w), lambda b, i, c=col0 // w: (b * nt + i, c))

    def prev(col0):
        return pl.BlockSpec((16, w), lambda b, i, c=col0 // w: (jnp.maximum((b * nt + i) * hb - 1, 0), c))

    def nxt(col0):
        return pl.BlockSpec((16, w), lambda b, i, c=col0 // w: (jnp.minimum((b * nt + i + 1) * hb, last_hb), c))

    def meta(col0):
        return pl.BlockSpec((N_META, w), lambda b, i, c=col0 // w: (0, c))

    vec = pl.BlockSpec((1, w), lambda b, i: (0, 0))
    return pl.pallas_call(
        _conv_kernel,
        grid=(BATCH, nt),
        in_specs=[
            main(COL_GV), main(COL_GG), main(COL_ZB),
            prev(COL_GV), prev(COL_GG), nxt(COL_GV), nxt(COL_GG),
            meta(COL_GV), meta(COL_GG), meta(COL_ZB),
            pl.BlockSpec((CONV_KERNEL, w), lambda b, i: (0, 0)), vec, vec, vec,
        ],
        out_specs=[
            pl.BlockSpec((tt, w), lambda b, i: (b * nt + i, 0)),
            pl.BlockSpec((N_META, w), lambda b, i: (b, 0)),
        ],
        out_shape=[
            jax.ShapeDtypeStruct((BATCH * SEQ, w), BF16),
            jax.ShapeDtypeStruct((BATCH * N_META, w), BF16),
        ],
        scratch_shapes=[pltpu.VMEM((CONV_UROWS, w), F32), pltpu.VMEM((7, CONV_SROWS, CONV_CC), F32),
                        pltpu.VMEM((tt, w), F32), pltpu.VMEM((N_META, w), F32),
                        pltpu.VMEM((CONV_KERNEL, 8, w), F32)],
        compiler_params=_cparams(("parallel", "arbitrary")),
        name="conformer_conv",
    )(proj_x, proj_x, proj_x, proj_x, proj_x, proj_x, proj_x, proj_m, proj_m, proj_m,
      dw_w, dw_b.reshape(1, w), ln_g.reshape(1, w), ln_b.reshape(1, w))


NA_WIN = NA_KH * GRID_W


NA_ROWS_PER_TRIP = 16
NA_LOOKAHEAD = 3
NA_TBL_W = 1024


def _natten_kernel(q_ref, k_ref, v_ref, z_ref, km_ref, vm_ref, rp_ref, o_ref, bias_ref):
    lane = lax.broadcasted_iota(jnp.int32, (1, 2 * NA_HEAD_DIM), 1)
    lane_lo = lane < NA_HEAD_DIM
    scale = NA_HEAD_DIM ** -0.5
    q_sel = (jnp.where(lane_lo, scale, 0.0).astype(BF16), jnp.where(lane_lo, 0.0, scale).astype(BF16))
    km = km_ref[...]
    vm = vm_ref[...]

    @pl.when(pl.program_id(1) == 0)
    def _():
        qcol = lax.broadcasted_iota(jnp.int32, (GRID_W, NA_TBL_W), 0)
        kcol = lax.rem(lax.broadcasted_iota(jnp.int32, (GRID_W, NA_TBL_W), 1), GRID_W)
        wstart = jnp.clip(qcol - NA_KW // 2, 0, GRID_W - NA_KW)
        in_window = jnp.logical_and(kcol >= wstart, kcol < wstart + NA_KW)
        for e in range(2):
            rows = jnp.broadcast_to(rp_ref[e], (GRID_W, NA_TBL_W))
            for par in range(2):
                shift = (NA_TBL_W - (NA_KW - 1) - par * GRID_W) % NA_TBL_W
                t = pltpu.roll(rows, shift, 1, stride=1, stride_axis=0)
                bias_ref[par, e * GRID_W:(e + 1) * GRID_W, :] = jnp.where(in_window, t, NEG_BIG)

    def scores(r):
        rs = jnp.clip(r - NA_KH // 2, 0, GRID_ROWS - NA_KH)
        cls = rs - r + NA_KH - 1
        par = lax.rem(cls, 2)
        boff = pl.multiple_of((cls - par) * GRID_W, 2 * GRID_W)
        qs = pl.multiple_of(r * GRID_W, GRID_W)
        ks = pl.multiple_of(rs * GRID_W, GRID_W)
        q = q_ref[pl.ds(qs, GRID_W), :]
        kw = k_ref[pl.ds(ks, NA_WIN), :]
        q2 = jnp.concatenate([q * q_sel[0], q * q_sel[1]], axis=0)
        s = _dot_nt(q2, kw) + bias_ref[par, :, pl.ds(boff, NA_WIN)]
        sm = _dot_nt(q2, km)
        return qs, ks, s, sm

    def softmax(s, sm):
        m = jnp.maximum(jnp.max(s, axis=-1, keepdims=True), jnp.max(sm, axis=-1, keepdims=True))
        p = jnp.exp(s - m)
        pm = jnp.exp(sm - m)
        den = jnp.sum(p, axis=-1, keepdims=True) + jnp.sum(pm, axis=-1, keepdims=True)
        return p.astype(BF16), pm.astype(BF16), den

    def finish(qs, ks, p, pm, den):
        o2 = (_dot(p, v_ref[pl.ds(ks, NA_WIN), :]) + _dot(pm, vm)) / den
        o = jnp.where(lane_lo, o2[:GRID_W], o2[GRID_W:])
        o = o * _silu(z_ref[pl.ds(qs, GRID_W), :].astype(F32))
        o_ref[pl.ds(qs, GRID_W), :] = o.astype(o_ref.dtype)

    def trip(i, carry):
        first = i * NA_ROWS_PER_TRIP
        pending = [scores(first + u) for u in range(NA_LOOKAHEAD)]
        for u in range(NA_ROWS_PER_TRIP):
            qs, ks, s, sm = pending.pop(0)
            p, pm, den = softmax(s, sm)
            if u + NA_LOOKAHEAD < NA_ROWS_PER_TRIP:
                pending.append(scores(first + u + NA_LOOKAHEAD))
            finish(qs, ks, p, pm, den)
        return carry

    lax.fori_loop(0, GRID_ROWS // NA_ROWS_PER_TRIP, trip, 0)


def _natten(proj_x, proj_m, rpb):
    pw = 2 * NA_HEAD_DIM
    npair = NA_HEADS // 2
    n_dr = 2 * NA_KH - 1
    rp = jnp.pad(rpb.astype(F32), ((0, 0), (0, 0), (0, GRID_W - (2 * NA_KW - 1))))
    rp = jnp.pad(rp.reshape(NA_HEADS, n_dr * GRID_W), ((0, 0), (0, NA_TBL_W - n_dr * GRID_W)))
    rp = rp.reshape(NA_HEADS, 1, NA_TBL_W)

    def xspec(col0):
        return pl.BlockSpec((SEQ, pw), lambda h, b, c=col0 // pw: (b, c + h))

    def mspec(col0):
        return pl.BlockSpec((N_META, pw), lambda h, b, c=col0 // pw: (b, c + h))

    return pl.pallas_call(
        _natten_kernel,
        grid=(npair, BATCH),
        in_specs=[
            xspec(0), xspec(NA_WIDTH), xspec(2 * NA_WIDTH), xspec(3 * NA_WIDTH),
            mspec(NA_WIDTH), mspec(2 * NA_WIDTH),
            pl.BlockSpec((2, 1, NA_TBL_W), lambda h, b: (h, 0, 0)),
        ],
        out_specs=pl.BlockSpec((SEQ, pw), lambda h, b: (b, h)),
        out_shape=jax.ShapeDtypeStruct((BATCH * SEQ, NA_WIDTH), BF16),
        scratch_shapes=[pltpu.VMEM((2, 2 * GRID_W, NA_TBL_W), F32)],
        compiler_params=_cparams(("arbitrary", "arbitrary")),
        name="natten",
    )(proj_x, proj_x, proj_x, proj_x, proj_m, proj_m, rp)


def kernel(x, meta_tokens, e_norm_g, e_w_in, e_conv_w, e_conv_b, e_dt_bias, e_A_log, e_D, e_ssd_norm_g, e_dw_w, e_dw_b, e_ln_g, e_ln_b, e_w_out, o_norm_g, o_w_in, o_rpb, o_w_out, final_norm_g):
    nb = x.shape[0]
    x2d = x.reshape(nb * SEQ, D_MODEL)
    meta = meta_tokens.astype(x.dtype)

    w_main, wdt = _w_prep(jnp.swapaxes(e_w_in, 1, 2)[0])

    proj_x, dt_x = _in_proj(x2d, e_norm_g[0], w_main, wdt, w_is_nk=True, name="in_proj0")
    proj_m, dt_m = _in_proj(meta, e_norm_g[0], w_main, wdt, w_is_nk=True, name="in_proj0_meta")

    def group_dt(a, rows):
        a = a[:, :2 * SSD_HEADS].reshape(-1, rows, 2, SSD_GROUPS, HEADS_PER_GROUP)
        return jnp.transpose(a, (0, 3, 2, 4, 1)).reshape(-1, SSD_GROUPS, 2 * HEADS_PER_GROUP, rows)

    dt_seq_t = jnp.concatenate([
        jnp.full((nb, SSD_GROUPS, 2 * HEADS_PER_GROUP, PAD_L), NEG_BIG, F32),
        jnp.broadcast_to(group_dt(dt_m, N_META), (nb, SSD_GROUPS, 2 * HEADS_PER_GROUP, N_META)),
        group_dt(dt_x, SEQ),
    ], axis=3)

    def group_vec(a):
        return jnp.transpose(a.reshape(2, SSD_GROUPS, HEADS_PER_GROUP), (1, 0, 2)).reshape(SSD_GROUPS, -1)

    bias_g = group_vec(e_dt_bias[0].astype(F32))
    alog_g = group_vec(e_A_log[0].astype(F32))
    dsk = jnp.repeat(e_D[0].astype(F32), SSD_HEAD_DIM).reshape(1, SSD_WIDTH)

    yg_x, yg_m, ss_x, ss_m = _ssd(
        proj_x, proj_m, dt_seq_t, e_conv_w[0], e_conv_b[0].reshape(1, -1),
        bias_g[:, :, None], alog_g[:, :, None], dsk)
    yb_x, yb_m = _conformer_conv(proj_x, proj_m, e_dw_w[0], e_dw_b[0], e_ln_g[0], e_ln_b[0])

    w_out = e_w_out[0].astype(BF16)
    h1_x = _out_proj0(x2d, yg_x, ss_x, e_ssd_norm_g[0], yb_x, w_out, name="out_proj0")
    meta_b = jnp.broadcast_to(meta[None], (nb, N_META, D_MODEL)).reshape(nb * N_META, D_MODEL)
    h1_m = _out_proj0(meta_b, yg_m, ss_m, e_ssd_norm_g[0], yb_m, w_out, name="out_proj0_meta")

    w_in1 = o_w_in[0].astype(BF16)
    p1_x = _in_proj(h1_x, o_norm_g[0], w_in1, name="in_proj1")
    p1_m = _in_proj(h1_m, o_norm_g[0], w_in1, name="in_proj1_meta")
    o_x = _natten(p1_x, p1_m, o_rpb[0])
    out = _out_proj1(h1_x, o_x, o_w_out[0].astype(BF16), final_norm_g, name="out_proj1")
    return out.reshape(nb, SEQ, D_MODEL)
```

```python
import functools

import jax
import jax.numpy as jnp
import numpy as np
from jax import lax
from jax.experimental import pallas as pl
from jax.experimental.pallas import tpu as pltpu

F32 = jnp.float32
BF16 = jnp.bfloat16

D_MODEL = 2048
BATCH = 4
SEQ = 2048
N_META = 16
GRID_W = 64
EPS = 1e-5

SSD_HEAD_DIM = 64
SSD_WIDTH = 2048
SSD_HEADS = 32
SSD_GROUPS = 8
SSD_STATE = 128
SSD_CONV = 5
CHUNK = 128
HEADS_PER_GROUP = SSD_HEADS // SSD_GROUPS
GROUP_W = HEADS_PER_GROUP * SSD_HEAD_DIM

CONV_WIDTH = 2048
CONV_KERNEL = 31
CONV_HALO = CONV_KERNEL // 2

NA_HEAD_DIM = 64
NA_WIDTH = 2048
NA_HEADS = 32
NA_KH = 8
NA_KW = 16
GRID_ROWS = SEQ // GRID_W

PAD_L = (-(N_META + SEQ)) % CHUNK
T_PAD = PAD_L + N_META + SEQ
N_CHUNKS = T_PAD // CHUNK
RAW_OFF = 8
SSD_CHUNKS_PER_TRIP = 2
NEG_BIG = -1e30
LOG2E = 1.4426950408889634

VMEM_LIMIT_BYTES = 56 * 1024 * 1024

COL_ZA = 0
COL_XS = 2048
COL_B = 4096
COL_C = 5120
COL_ZB = 6144
COL_GV = 8192
COL_GG = 10240
N_MAIN = 12288
DT_COLS = 128


def _cparams(sem):
    return pltpu.CompilerParams(dimension_semantics=sem, vmem_limit_bytes=VMEM_LIMIT_BYTES)


def _dot(a, b):
    return jnp.dot(a, b, preferred_element_type=F32)


def _dot_nt(a, b):
    return lax.dot_general(a, b, (((1,), (1,)), ((), ())), preferred_element_type=F32)


def _silu(x):
    return x * jax.nn.sigmoid(x)


def _softplus(x):
    return jnp.maximum(x, 0.0) + jnp.log1p(jnp.exp(-jnp.abs(x)))


def _split3(v):
    hi = v.astype(BF16)
    r1 = v - hi.astype(F32)
    mid = r1.astype(BF16)
    lo = (r1 - mid.astype(F32)).astype(BF16)
    return hi, mid, lo


W_PREP_ROWS = 1024


def _w_prep_kernel(w_ref, wdt_ref, o_ref, odt_ref):
    o_ref[...] = w_ref[...].astype(BF16)

    @pl.when(pl.program_id(0) == 0)
    def _():
        w = wdt_ref[...]
        hi = w.astype(BF16)
        lo = (w - hi.astype(F32)).astype(BF16)
        pad = jnp.zeros((DT_COLS - 2 * SSD_HEADS, w.shape[1]), BF16)
        odt_ref[...] = jnp.concatenate([hi, pad, lo, pad], axis=0)


def _w_prep(w_t):
    n_all, d = w_t.shape
    ndt = 2 * SSD_HEADS
    dt0 = COL_ZB
    tr = W_PREP_ROWS
    return pl.pallas_call(
        _w_prep_kernel,
        grid=(N_MAIN // tr,),
        in_specs=[
            pl.BlockSpec((pl.Element(tr), pl.Element(d)),
                         lambda j: ((j * (tr // ndt) + jnp.where(j >= dt0 // tr, 1, 0)) * ndt, 0)),
            pl.BlockSpec((ndt, d), lambda j: (dt0 // ndt, 0)),
        ],
        out_specs=[
            pl.BlockSpec((tr, d), lambda j: (j, 0)),
            pl.BlockSpec((2 * DT_COLS, d), lambda j: (0, 0)),
        ],
        out_shape=[
            jax.ShapeDtypeStruct((N_MAIN, d), BF16),
            jax.ShapeDtypeStruct((2 * DT_COLS, d), BF16),
        ],
        compiler_params=_cparams(("arbitrary",)),
        name="w_in_prep",
    )(w_t, w_t)


def _in_proj_kernel(x_ref, g_ref, w_ref, *rest, with_dt, w_is_nk):
    if with_dt:
        wdt_ref, o_ref, dt_ref, u_ref = rest
    else:
        o_ref, u_ref = rest

    @pl.when(pl.program_id(1) == 0)
    def _():
        x = x_ref[...]
        ms = jnp.mean(x * x, axis=-1, keepdims=True)
        y = x * lax.rsqrt(ms + EPS) * g_ref[...]
        u = y.astype(BF16)
        u_ref[...] = u
        if with_dt:
            u_lo = (y - u.astype(F32)).astype(BF16)
            w_hi = wdt_ref[:DT_COLS, :]
            w_lo = wdt_ref[DT_COLS:, :]
            dt_ref[...] = _dot_nt(u, w_hi) + _dot_nt(u_lo, w_hi) + _dot_nt(u, w_lo)

    mm = _dot_nt if w_is_nk else _dot
    o_ref[...] = mm(u_ref[...], w_ref[...]).astype(o_ref.dtype)


def _in_proj(x2d, g, w, wdt=None, *, w_is_nk=False, name):
    m, d = x2d.shape
    n = w.shape[0] if w_is_nk else w.shape[1]
    tm = min(m, 1024)
    tn = 1024
    with_dt = wdt is not None
    in_specs = [
        pl.BlockSpec((tm, d), lambda i, j: (i, 0)),
        pl.BlockSpec((1, d), lambda i, j: (0, 0)),
        pl.BlockSpec((tn, d), lambda i, j: (j, 0)) if w_is_nk else pl.BlockSpec((d, tn), lambda i, j: (0, j)),
    ]
    out_shape = [jax.ShapeDtypeStruct((m, n), BF16)]
    out_specs = [pl.BlockSpec((tm, tn), lambda i, j: (i, j))]
    args = [x2d, g.reshape(1, d), w]
    if with_dt:
        in_specs.append(pl.BlockSpec((2 * DT_COLS, d), lambda i, j: (0, 0)))
        out_shape.append(jax.ShapeDtypeStruct((m, DT_COLS), F32))
        out_specs.append(pl.BlockSpec((tm, DT_COLS), lambda i, j: (i, 0)))
        args.append(wdt)
    res = pl.pallas_call(
        functools.partial(_in_proj_kernel, with_dt=with_dt, w_is_nk=w_is_nk),
        grid=(m // tm, n // tn),
        in_specs=in_specs,
        out_specs=out_specs,
        out_shape=out_shape,
        scratch_shapes=[pltpu.VMEM((tm, d), BF16)],
        compiler_params=_cparams(("parallel", "arbitrary")),
        name=name,
    )(*args)
    return res if with_dt else res[0]


def _out_proj0_kernel(h_ref, ya_ref, ss_ref, ng_ref, yb_ref, w_ref, o_ref):
    r = lax.rsqrt(ss_ref[:, 0:1] * (1.0 / SSD_WIDTH) + EPS)
    yan = (ya_ref[...].astype(F32) * r * ng_ref[...]).astype(BF16)
    o_ref[...] = h_ref[...] + _dot(yan, w_ref[:SSD_WIDTH, :]) + _dot(yb_ref[...], w_ref[SSD_WIDTH:, :])


def _out_proj0(h2d, ya, ss, ng, yb, w_out, *, name):
    m, d = h2d.shape
    tm = min(m, 256)
    return pl.pallas_call(
        _out_proj0_kernel,
        grid=(m // tm,),
        in_specs=[
            pl.BlockSpec((tm, d), lambda i: (i, 0)),
            pl.BlockSpec((tm, SSD_WIDTH), lambda i: (i, 0)),
            pl.BlockSpec((tm, 128), lambda i: (i, 0)),
            pl.BlockSpec((1, SSD_WIDTH), lambda i: (0, 0)),
            pl.BlockSpec((tm, CONV_WIDTH), lambda i: (i, 0)),
            pl.BlockSpec((SSD_WIDTH + CONV_WIDTH, d), lambda i: (0, 0), pipeline_mode=pl.Buffered(1)),
        ],
        out_specs=pl.BlockSpec((tm, d), lambda i: (i, 0)),
        out_shape=jax.ShapeDtypeStruct((m, d), F32),
        compiler_params=_cparams(("parallel",)),
        name=name,
    )(h2d, ya, ss, ng.reshape(1, SSD_WIDTH), yb, w_out)


def _out_proj1_kernel(h_ref, o_ref, w_ref, g_ref, out_ref):
    hn = h_ref[...] + _dot(o_ref[...], w_ref[...])
    ms = jnp.mean(hn * hn, axis=-1, keepdims=True)
    out_ref[...] = hn * lax.rsqrt(ms + EPS) * g_ref[...]


def _out_proj1(h2d, o, w, g, *, name):
    m, d = h2d.shape
    tm = 512
    return pl.pallas_call(
        _out_proj1_kernel,
        grid=(m // tm,),
        in_specs=[
            pl.BlockSpec((tm, d), lambda i: (i, 0)),
            pl.BlockSpec((tm, NA_WIDTH), lambda i: (i, 0)),
            pl.BlockSpec((NA_WIDTH, d), lambda i: (0, 0)),
            pl.BlockSpec((1, d), lambda i: (0, 0)),
        ],
        out_specs=pl.BlockSpec((tm, d), lambda i: (i, 0)),
        out_shape=jax.ShapeDtypeStruct((m, d), F32),
        compiler_params=_cparams(("parallel",)),
        name=name,
    )(h2d, o, w, g.reshape(1, d))


def _ssd_kernel(za_x, xs_x, bm_x, cm_x, za_m, xs_m, bm_m, cm_m, dtt_ref,
                cwx, cwb, cwc, cbx, cbb, cbc, biast, alogt, dsk,
                yg_x, yg_m, ss_x, ss_m,
                raw_x, raw_b, raw_c, xc, bc, cc, ybuf_f, ybuf_b, csd_t, at, cs_t, st_f, st_b):
    g = pl.program_id(1)

    for raw, m_ref, x_ref in ((raw_x, xs_m, xs_x), (raw_b, bm_m, bm_x), (raw_c, cm_m, cm_x)):
        w = raw.shape[1]
        raw[0:RAW_OFF + PAD_L, :] = jnp.zeros((RAW_OFF + PAD_L, w), F32)
        raw[RAW_OFF + PAD_L:RAW_OFF + CHUNK, :] = m_ref[...].astype(F32)
        raw[RAW_OFF + CHUNK:RAW_OFF + T_PAD, :] = x_ref[...].astype(F32)
        raw[RAW_OFF + T_PAD:RAW_OFF + T_PAD + 8, :] = jnp.zeros((8, w), F32)

    row = lax.broadcasted_iota(jnp.int32, (CHUNK, 1), 0)

    def conv_chunk(c, carry):
        base = pl.multiple_of(c * CHUNK, CHUNK)
        keep = jnp.logical_or(c > 0, row >= PAD_L)
        for raw, cw, cb, dst in ((raw_x, cwx, cbx, xc), (raw_b, cwb, cbb, bc), (raw_c, cwc, cbc, cc)):
            win = raw[pl.ds(base, CHUNK + 16), :]
            acc = cb[...]
            for k in range(SSD_CONV):
                s0 = RAW_OFF - SSD_CONV // 2 + k
                acc = acc + win[s0:s0 + CHUNK, :] * cw[k:k + 1, :]
            v = jnp.where(keep, _silu(acc), 0.0)
            dst[pl.ds(base, CHUNK), :] = v.astype(dst.dtype)
        return carry

    lax.fori_loop(0, N_CHUNKS, conv_chunk, 0)

    a_t = -jnp.exp(alogt[0])
    dtv = _softplus(dtt_ref[0, 0] + biast[0])
    at[...] = dtv * a_t
    csd_t[...] = -jnp.log2(dtv)

    li = lax.broadcasted_iota(jnp.int32, (CHUNK, CHUNK), 0)
    si = lax.broadcasted_iota(jnp.int32, (CHUNK, CHUNK), 1)
    lane = lax.broadcasted_iota(jnp.int32, (1, CHUNK), 1)
    lane_lo = lane < SSD_HEAD_DIM
    head_sel = (jnp.where(lane_lo, 1.0, 0.0).astype(BF16), jnp.where(lane_lo, 0.0, 1.0).astype(BF16))
    masks = (li >= si, li <= si)
    ends = (CHUNK - 1, 0)

    a2 = jnp.concatenate([at[:, c * CHUNK:(c + 1) * CHUNK] for c in range(N_CHUNKS)]
                         + [jnp.zeros((8, CHUNK), F32)], axis=0)
    pieces = _split3(a2)
    t_f = jnp.where(li <= si, 1.0, 0.0).astype(BF16)
    t_b = jnp.where(li >= si, 1.0, 0.0).astype(BF16)
    cs_f = sum(_dot(p, t_f) for p in pieces)
    cs_b = sum(_dot(p, t_b) for p in pieces)
    is_fwd = lax.rem(lax.broadcasted_iota(jnp.int32, (8 * (N_CHUNKS + 1), 1), 0), 8) < HEADS_PER_GROUP
    cs2 = jnp.where(is_fwd, cs_f, cs_b) * LOG2E
    for c in range(N_CHUNKS):
        cols = slice(c * CHUNK, (c + 1) * CHUNK)
        cs_t[:, cols] = cs2[8 * c:8 * c + 8]
        csd_t[:, cols] = csd_t[:, cols] + cs2[8 * c:8 * c + 8]

    st_f[...] = jnp.zeros_like(st_f)
    st_b[...] = jnp.zeros_like(st_b)

    def chunk_inputs(c):
        base = c * CHUNK if isinstance(c, int) else pl.multiple_of(c * CHUNK, CHUNK)
        xb = xc[pl.ds(base, CHUNK), :].astype(BF16)
        bb = bc[pl.ds(base, CHUNK), :]
        cb_ = cc[pl.ds(base, CHUNK), :]
        csr8 = cs_t[:, pl.ds(base, CHUNK)]
        csd8 = csd_t[:, pl.ds(base, CHUNK)]
        cbm = _dot_nt(cb_, bb)
        bt = bb.astype(F32).T
        csc8 = csr8.T
        return base, xb, cb_.astype(F32), csr8, csd8, cbm, bt, csc8

    def operands(d, inp):
        _, xb, cf, csr8, csd8, cbm, bt, csc8 = inp
        out = []
        for pair in range(2):
            xp = xb[:, pair * CHUNK:(pair + 1) * CHUNK]
            per_head = []
            decs = []
            for e in range(2):
                hd = 4 * d + 2 * pair + e
                xk = xp * head_sel[e]
                csb = jnp.broadcast_to(csc8[:, hd:hd + 1], (CHUNK, CHUNK))
                csd = csd8[hd:hd + 1, :]
                lmat = jnp.exp2(jnp.where(masks[d], csb - csd, NEG_BIG))
                mh = (cbm * lmat).astype(BF16)
                csc = (cf * jnp.exp2(csb)).astype(BF16)
                wend = csr8[hd:hd + 1, ends[d]:ends[d] + 1]
                bs = (bt * jnp.exp2(wend - csd)).astype(BF16)
                per_head.append((mh, csc, bs, xk))
                decs.append(jnp.exp2(wend))
            (mh0, csc0, bs0, xk0), (mh1, csc1, bs1, xk1) = per_head
            lhs_y = jnp.concatenate([mh0, mh1, csc0, csc1], axis=1)
            lhs_s = jnp.concatenate([bs0, bs1], axis=1)
            x_cat = jnp.concatenate([xk0, xk1], axis=0)
            out.append((lhs_y, lhs_s, x_cat, jnp.where(lane_lo, decs[0], decs[1])))
        return out

    def outputs(state, ybuf, base, ops):
        new_state = []
        for pair, (lhs_y, lhs_s, x_cat, dec) in enumerate(ops):
            ps = slice(pair * CHUNK, (pair + 1) * CHUNK)
            sp = state[:, ps]
            spb = sp.astype(BF16)
            rhs_y = jnp.concatenate([x_cat, spb * head_sel[0], spb * head_sel[1]], axis=0)
            ybuf[pl.ds(base, CHUNK), ps] = _dot(lhs_y, rhs_y)
            new_state.append(sp * dec + _dot(lhs_s, x_cat))
        return jnp.concatenate(new_state, axis=1)

    def scan_trip(chunks_f, chunks_b):
        inps_f = [chunk_inputs(c) for c in chunks_f]
        inps_b = [chunk_inputs(c) for c in chunks_b]
        ops_f = [operands(0, inp) for inp in inps_f]
        ops_b = [operands(1, inp) for inp in inps_b]
        sf = st_f[...]
        sb = st_b[...]
        for inp_f, op_f, inp_b, op_b in zip(inps_f, ops_f, inps_b, ops_b):
            sf = outputs(sf, ybuf_f, inp_f[0], op_f)
            sb = outputs(sb, ybuf_b, inp_b[0], op_b)
        st_f[...] = sf
        st_b[...] = sb

    def scan_step(i, carry):
        first = i * SSD_CHUNKS_PER_TRIP
        scan_trip([first + u for u in range(SSD_CHUNKS_PER_TRIP)],
                  [N_CHUNKS - 1 - first - u for u in range(SSD_CHUNKS_PER_TRIP)])
        return carry

    n_trips = N_CHUNKS // SSD_CHUNKS_PER_TRIP
    lax.fori_loop(0, n_trips, scan_step, 0)
    for c in range(n_trips * SSD_CHUNKS_PER_TRIP, N_CHUNKS):
        scan_trip([c], [N_CHUNKS - 1 - c])

    @pl.when(g == 0)
    def _():
        ss_x[...] = jnp.zeros_like(ss_x)
        ss_m[...] = jnp.zeros_like(ss_m)

    def finish(i, carry):
        for u in range(2):
            base = pl.multiple_of((2 * i + u) * CHUNK, CHUNK)
            rows = pl.ds(CHUNK + base, CHUNK)
            y = ybuf_f[rows, :] + ybuf_b[rows, :] + dsk[...] * xc[rows, :]
            yg = y * _silu(za_x[pl.ds(base, CHUNK), :].astype(F32))
            yg_x[pl.ds(base, CHUNK), :] = yg.astype(yg_x.dtype)
            ss_x[pl.ds(base, CHUNK), :] = ss_x[pl.ds(base, CHUNK), :] + jnp.sum(yg * yg, axis=-1, keepdims=True)
        return carry

    lax.fori_loop(0, SEQ // (2 * CHUNK), finish, 0)

    y = ybuf_f[PAD_L:CHUNK, :] + ybuf_b[PAD_L:CHUNK, :] + dsk[...] * xc[PAD_L:CHUNK, :]
    yg = y * _silu(za_m[...].astype(F32))
    yg_m[...] = yg.astype(yg_m.dtype)
    ss_m[...] = ss_m[...] + jnp.sum(yg * yg, axis=-1, keepdims=True)


def _ssd(proj_x, proj_m, dtt, conv_w, conv_b, biast, alogt, dsk):
    gw, sn = GROUP_W, SSD_STATE
    nb = BATCH

    def xspec(width, col0):
        return pl.BlockSpec((SEQ, width), lambda b, g, c=col0 // width: (b, c + g))

    def mspec(width, col0):
        return pl.BlockSpec((N_META, width), lambda b, g, c=col0 // width: (0, c + g))

    def wspec(rows, width, col0):
        return pl.BlockSpec((rows, width), lambda b, g, c=col0 // width: (0, c + g))

    in_specs = [
        xspec(gw, COL_ZA), xspec(gw, COL_XS), xspec(sn, COL_B), xspec(sn, COL_C),
        mspec(gw, COL_ZA), mspec(gw, COL_XS), mspec(sn, COL_B), mspec(sn, COL_C),
        pl.BlockSpec((1, 1, 8, T_PAD), lambda b, g: (b, g, 0, 0)),
        wspec(SSD_CONV, gw, 0), wspec(SSD_CONV, sn, SSD_WIDTH), wspec(SSD_CONV, sn, SSD_WIDTH + SSD_GROUPS * sn),
        wspec(1, gw, 0), wspec(1, sn, SSD_WIDTH), wspec(1, sn, SSD_WIDTH + SSD_GROUPS * sn),
        pl.BlockSpec((1, 8, 1), lambda b, g: (g, 0, 0)),
        pl.BlockSpec((1, 8, 1), lambda b, g: (g, 0, 0)),
        pl.BlockSpec((1, gw), lambda b, g: (0, g)),
    ]
    out_shape = [
        jax.ShapeDtypeStruct((nb * SEQ, SSD_WIDTH), BF16),
        jax.ShapeDtypeStruct((nb * N_META, SSD_WIDTH), BF16),
        jax.ShapeDtypeStruct((nb * SEQ, 128), F32),
        jax.ShapeDtypeStruct((nb * N_META, 128), F32),
    ]
    out_specs = [
        pl.BlockSpec((SEQ, gw), lambda b, g: (b, g)),
        pl.BlockSpec((N_META, gw), lambda b, g: (b, g)),
        pl.BlockSpec((SEQ, 128), lambda b, g: (b, 0)),
        pl.BlockSpec((N_META, 128), lambda b, g: (b, 0)),
    ]
    raw_rows = RAW_OFF + T_PAD + 8
    scratch = [
        pltpu.VMEM((raw_rows, gw), F32), pltpu.VMEM((raw_rows, sn), F32), pltpu.VMEM((raw_rows, sn), F32),
        pltpu.VMEM((T_PAD, gw), F32), pltpu.VMEM((T_PAD, sn), BF16), pltpu.VMEM((T_PAD, sn), BF16),
        pltpu.VMEM((T_PAD, gw), F32), pltpu.VMEM((T_PAD, gw), F32),
        pltpu.VMEM((8, T_PAD), F32), pltpu.VMEM((8, T_PAD), F32), pltpu.VMEM((8, T_PAD), F32),
        pltpu.VMEM((sn, gw), F32), pltpu.VMEM((sn, gw), F32),
    ]
    return pl.pallas_call(
        _ssd_kernel,
        grid=(nb, SSD_GROUPS),
        in_specs=in_specs,
        out_specs=out_specs,
        out_shape=out_shape,
        scratch_shapes=scratch,
        compiler_params=_cparams(("parallel", "arbitrary")),
        name="ssd_mixer",
    )(proj_x, proj_x, proj_x, proj_x, proj_m, proj_m, proj_m, proj_m, dtt,
      conv_w, conv_w, conv_w, conv_b, conv_b, conv_b, biast, alogt, dsk)


CONV_TT = 512
CONV_RC = 64
CONV_GR = 16
CONV_CC = 128
CONV_LW = 128
CONV_D0 = 1
CONV_SB = 64
CONV_SROWS = 576
CONV_UROWS = CONV_SROWS + 8


def _conv_kernel(gv, gg, zb, gv_p, gg_p, gv_n, gg_n, gv_m, gg_m, zb_m, dww, dwb, lng, lnb,
                 yb_x, yb_m, ubuf, sh, cbuf, cbuf_m, wrep):
    i = pl.program_id(1)
    nt = pl.num_programs(1)
    tt = CONV_TT

    def glu(v_ref, g_ref):
        return v_ref[...].astype(F32) * jax.nn.sigmoid(g_ref[...].astype(F32))

    ubuf[0:16, :] = jnp.zeros((16, CONV_WIDTH), F32)
    ubuf[48 + tt:, :] = jnp.zeros((CONV_UROWS - 48 - tt, CONV_WIDTH), F32)
    ubuf[32:32 + tt, :] = glu(gv, gg)

    @pl.when(i == 0)
    def _():
        ubuf[16:32, :] = glu(gv_m, gg_m)

    @pl.when(i > 0)
    def _():
        ubuf[16:32, :] = glu(gv_p, gg_p)

    @pl.when(i == nt - 1)
    def _():
        ubuf[32 + tt:48 + tt, :] = jnp.zeros((16, CONV_WIDTH), F32)

    @pl.when(i < nt - 1)
    def _():
        ubuf[32 + tt:48 + tt, :] = glu(gv_n, gg_n)

    def conv_rows(row0, nrows, cs):
        outs = []
        for l0 in range(0, cs.stop - cs.start, CONV_LW):
            ls = slice(l0, l0 + CONV_LW)
            gs = slice(cs.start + l0, cs.start + l0 + CONV_LW)
            acc = jnp.broadcast_to(dwb[:, gs], (nrows, CONV_LW))
            for r in range(8):
                qs = [q for q in range(5) if 0 <= 8 * q + r - CONV_D0 < CONV_KERNEL]
                span = pl.ds(row0 + 8 * qs[0], 8 * (qs[-1] - qs[0]) + nrows)
                blk = ubuf[span, gs] if r == 0 else sh[r - 1, span, ls]
                for q in qs:
                    k = 8 * q + r - CONV_D0
                    tap = blk[8 * (q - qs[0]):8 * (q - qs[0]) + nrows, :]
                    acc = acc + tap * pltpu.repeat(wrep[k, :, gs], nrows // 8, axis=0)
            outs.append(acc)
        return jnp.concatenate(outs, axis=1)

    def norm_gate(c_rows, z_rows):
        n = c_rows.shape[-1]
        mu = jnp.sum(c_rows, axis=-1, keepdims=True) * (1.0 / n)
        var = jnp.sum(c_rows * c_rows, axis=-1, keepdims=True) * (1.0 / n) - mu * mu
        yn = (c_rows - mu) * lax.rsqrt(var + EPS) * lng[...] + lnb[...]
        return _silu(yn) * _silu(z_rows)

    for k in range(CONV_KERNEL):
        wrep[k] = jnp.broadcast_to(dww[k:k + 1, :], (8, CONV_WIDTH))

    for cc in range(CONV_WIDTH // CONV_CC):
        cs = slice(cc * CONV_CC, (cc + 1) * CONV_CC)

        def shift_block(rb, carry, cs=cs):
            base = pl.multiple_of(rb * CONV_SB, CONV_SB)
            win = ubuf[pl.ds(base, CONV_SB + 8), cs]
            for r in range(1, 8):
                sh[r - 1, pl.ds(base, CONV_SB), :] = win[r:r + CONV_SB, :]
            return carry

        lax.fori_loop(0, CONV_SROWS // CONV_SB, shift_block, 0)

        def conv_chunk(rc, carry, cs=cs):
            base = pl.multiple_of(rc * CONV_RC, CONV_RC)
            cbuf[pl.ds(base, CONV_RC), cs] = conv_rows(base + 16, CONV_RC, cs)
            return carry

        lax.fori_loop(0, tt // CONV_RC, conv_chunk, 0)

        @pl.when(i == 0)
        def _(cs=cs):
            cbuf_m[:, cs] = conv_rows(0, N_META, cs)

    def gate_chunk(rc, carry):
        for half in range(2):
            base = pl.multiple_of(rc * (2 * CONV_GR) + half * CONV_GR, CONV_GR)
            rows = pl.ds(base, CONV_GR)
            out = norm_gate(cbuf[rows, :], zb[rows, :].astype(F32))
            yb_x[rows, :] = out.astype(yb_x.dtype)
        return carry

    lax.fori_loop(0, tt // (2 * CONV_GR), gate_chunk, 0)

    @pl.when(i == 0)
    def _():
        out = norm_gate(cbuf_m[...], zb_m[...].astype(F32))
        yb_m[...] = out.astype(yb_m.dtype)


def _conformer_conv(proj_x, proj_m, dw_w, dw_b, ln_g, ln_b):
    tt = CONV_TT
    nt = SEQ // tt
    w = CONV_WIDTH
    hb = tt // 16
    last_hb = BATCH * SEQ // 16 - 1

    def main(col0):
        return pl.BlockSpec((tt, w), lambda b, i, c=col0 // w: (b * nt + i, c))

    def prev(col0):
        return pl.BlockSpec((16, w), lambda b, i, c=col0 // w: (jnp.maximum((b * nt + i) * hb - 1, 0), c))

    def nxt(col0):
        return pl.BlockSpec((16, w), lambda b, i, c=col0 // w: (jnp.minimum((b * nt + i + 1) * hb, last_hb), c))

    def meta(col0):
        return pl.BlockSpec((N_META, w), lambda b, i, c=col0 // w: (0, c))

    vec = pl.BlockSpec((1, w), lambda b, i: (0, 0))
    return pl.pallas_call(
        _conv_kernel,
        grid=(BATCH, nt),
        in_specs=[
            main(COL_GV), main(COL_GG), main(COL_ZB),
            prev(COL_GV), prev(COL_GG), nxt(COL_GV), nxt(COL_GG),
            meta(COL_GV), meta(COL_GG), meta(COL_ZB),
            pl.BlockSpec((CONV_KERNEL, w), lambda b, i: (0, 0)), vec, vec, vec,
        ],
        out_specs=[
            pl.BlockSpec((tt, w), lambda b, i: (b * nt + i, 0)),
            pl.BlockSpec((N_META, w), lambda b, i: (b, 0)),
        ],
        out_shape=[
            jax.ShapeDtypeStruct((BATCH * SEQ, w), BF16),
            jax.ShapeDtypeStruct((BATCH * N_META, w), BF16),
        ],
        scratch_shapes=[pltpu.VMEM((CONV_UROWS, w), F32), pltpu.VMEM((7, CONV_SROWS, CONV_CC), F32),
                        pltpu.VMEM((tt, w), F32), pltpu.VMEM((N_META, w), F32),
                        pltpu.VMEM((CONV_KERNEL, 8, w), F32)],
        compiler_params=_cparams(("parallel", "arbitrary")),
        name="conformer_conv",
    )(proj_x, proj_x, proj_x, proj_x, proj_x, proj_x, proj_x, proj_m, proj_m, proj_m,
      dw_w, dw_b.reshape(1, w), ln_g.reshape(1, w), ln_b.reshape(1, w))


NA_WIN = NA_KH * GRID_W


NA_ROWS_PER_TRIP = 16
NA_LOOKAHEAD = 3
NA_TBL_W = 1024


def _natten_kernel(q_ref, k_ref, v_ref, z_ref, km_ref, vm_ref, rp_ref, o_ref, bias_ref):
    lane = lax.broadcasted_iota(jnp.int32, (1, 2 * NA_HEAD_DIM), 1)
    lane_lo = lane < NA_HEAD_DIM
    scale = NA_HEAD_DIM ** -0.5
    q_sel = (jnp.where(lane_lo, scale, 0.0).astype(BF16), jnp.where(lane_lo, 0.0, scale).astype(BF16))
    km = km_ref[...]
    vm = vm_ref[...]

    @pl.when(pl.program_id(1) == 0)
    def _():
        qcol = lax.broadcasted_iota(jnp.int32, (GRID_W, NA_TBL_W), 0)
        kcol = lax.rem(lax.broadcasted_iota(jnp.int32, (GRID_W, NA_TBL_W), 1), GRID_W)
        wstart = jnp.clip(qcol - NA_KW // 2, 0, GRID_W - NA_KW)
        in_window = jnp.logical_and(kcol >= wstart, kcol < wstart + NA_KW)
        for e in range(2):
            rows = jnp.broadcast_to(rp_ref[e], (GRID_W, NA_TBL_W))
            for par in range(2):
                shift = (NA_TBL_W - (NA_KW - 1) - par * GRID_W) % NA_TBL_W
                t = pltpu.roll(rows, shift, 1, stride=1, stride_axis=0)
                bias_ref[par, e * GRID_W:(e + 1) * GRID_W, :] = jnp.where(in_window, t, NEG_BIG)

    def scores(r):
        rs = jnp.clip(r - NA_KH // 2, 0, GRID_ROWS - NA_KH)
        cls = rs - r + NA_KH - 1
        par = lax.rem(cls, 2)
        boff = pl.multiple_of((cls - par) * GRID_W, 2 * GRID_W)
        qs = pl.multiple_of(r * GRID_W, GRID_W)
        ks = pl.multiple_of(rs * GRID_W, GRID_W)
        q = q_ref[pl.ds(qs, GRID_W), :]
        kw = k_ref[pl.ds(ks, NA_WIN), :]
        q2 = jnp.concatenate([q * q_sel[0], q * q_sel[1]], axis=0)
        s = _dot_nt(q2, kw) + bias_ref[par, :, pl.ds(boff, NA_WIN)]
        sm = _dot_nt(q2, km)
        return qs, ks, s, sm

    def softmax(s, sm):
        m = jnp.maximum(jnp.max(s, axis=-1, keepdims=True), jnp.max(sm, axis=-1, keepdims=True))
        p = jnp.exp(s - m)
        pm = jnp.exp(sm - m)
        den = jnp.sum(p, axis=-1, keepdims=True) + jnp.sum(pm, axis=-1, keepdims=True)
        return p.astype(BF16), pm.astype(BF16), den

    def finish(qs, ks, p, pm, den):
        o2 = (_dot(p, v_ref[pl.ds(ks, NA_WIN), :]) + _dot(pm, vm)) / den
        o = jnp.where(lane_lo, o2[:GRID_W], o2[GRID_W:])
        o = o * _silu(z_ref[pl.ds(qs, GRID_W), :].astype(F32))
        o_ref[pl.ds(qs, GRID_W), :] = o.astype(o_ref.dtype)

    def trip(i, carry):
        first = i * NA_ROWS_PER_TRIP
        pending = [scores(first + u) for u in range(NA_LOOKAHEAD)]
        for u in range(NA_ROWS_PER_TRIP):
            qs, ks, s, sm = pending.pop(0)
            p, pm, den = softmax(s, sm)
            if u + NA_LOOKAHEAD < NA_ROWS_PER_TRIP:
                pending.append(scores(first + u + NA_LOOKAHEAD))
            finish(qs, ks, p, pm, den)
        return carry

    lax.fori_loop(0, GRID_ROWS // NA_ROWS_PER_TRIP, trip, 0)


def _natten(proj_x, proj_m, rpb):
    pw = 2 * NA_HEAD_DIM
    npair = NA_HEADS // 2
    n_dr = 2 * NA_KH - 1
    rp = jnp.pad(rpb.astype(F32), ((0, 0), (0, 0), (0, GRID_W - (2 * NA_KW - 1))))
    rp = jnp.pad(rp.reshape(NA_HEADS, n_dr * GRID_W), ((0, 0), (0, NA_TBL_W - n_dr * GRID_W)))
    rp = rp.reshape(NA_HEADS, 1, NA_TBL_W)

    def xspec(col0):
        return pl.BlockSpec((SEQ, pw), lambda h, b, c=col0 // pw: (b, c + h))

    def mspec(col0):
        return pl.BlockSpec((N_META, pw), lambda h, b, c=col0 // pw: (b, c + h))

    return pl.pallas_call(
        _natten_kernel,
        grid=(npair, BATCH),
        in_specs=[
            xspec(0), xspec(NA_WIDTH), xspec(2 * NA_WIDTH), xspec(3 * NA_WIDTH),
            mspec(NA_WIDTH), mspec(2 * NA_WIDTH),
            pl.BlockSpec((2, 1, NA_TBL_W), lambda h, b: (h, 0, 0)),
        ],
        out_specs=pl.BlockSpec((SEQ, pw), lambda h, b: (b, h)),
        out_shape=jax.ShapeDtypeStruct((BATCH * SEQ, NA_WIDTH), BF16),
        scratch_shapes=[pltpu.VMEM((2, 2 * GRID_W, NA_TBL_W), F32)],
        compiler_params=_cparams(("arbitrary", "arbitrary")),
        name="natten",
    )(proj_x, proj_x, proj_x, proj_x, proj_m, proj_m, rp)


def kernel(x, meta_tokens, e_norm_g, e_w_in, e_conv_w, e_conv_b, e_dt_bias, e_A_log, e_D, e_ssd_norm_g, e_dw_w, e_dw_b, e_ln_g, e_ln_b, e_w_out, o_norm_g, o_w_in, o_rpb, o_w_out, final_norm_g):
    nb = x.shape[0]
    x2d = x.reshape(nb * SEQ, D_MODEL)
    meta = meta_tokens.astype(x.dtype)

    w_main, wdt = _w_prep(jnp.swapaxes(e_w_in, 1, 2)[0])

    proj_x, dt_x = _in_proj(x2d, e_norm_g[0], w_main, wdt, w_is_nk=True, name="in_proj0")
    proj_m, dt_m = _in_proj(meta, e_norm_g[0], w_main, wdt, w_is_nk=True, name="in_proj0_meta")

    def group_dt(a, rows):
        a = a[:, :2 * SSD_HEADS].reshape(-1, rows, 2, SSD_GROUPS, HEADS_PER_GROUP)
        return jnp.transpose(a, (0, 3, 2, 4, 1)).reshape(-1, SSD_GROUPS, 2 * HEADS_PER_GROUP, rows)

    dt_seq_t = jnp.concatenate([
        jnp.full((nb, SSD_GROUPS, 2 * HEADS_PER_GROUP, PAD_L), NEG_BIG, F32),
        jnp.broadcast_to(group_dt(dt_m, N_META), (nb, SSD_GROUPS, 2 * HEADS_PER_GROUP, N_META)),
        group_dt(dt_x, SEQ),
    ], axis=3)

    def group_vec(a):
        return jnp.transpose(a.reshape(2, SSD_GROUPS, HEADS_PER_GROUP), (1, 0, 2)).reshape(SSD_GROUPS, -1)

    bias_g = group_vec(e_dt_bias[0].astype(F32))
    alog_g = group_vec(e_A_log[0].astype(F32))
    dsk = jnp.repeat(e_D[0].astype(F32), SSD_HEAD_DIM).reshape(1, SSD_WIDTH)

    yg_x, yg_m, ss_x, ss_m = _ssd(
        proj_x, proj_m, dt_seq_t, e_conv_w[0], e_conv_b[0].reshape(1, -1),
        bias_g[:, :, None], alog_g[:, :, None], dsk)
    yb_x, yb_m = _conformer_conv(proj_x, proj_m, e_dw_w[0], e_dw_b[0], e_ln_g[0], e_ln_b[0])

    w_out = e_w_out[0].astype(BF16)
    h1_x = _out_proj0(x2d, yg_x, ss_x, e_ssd_norm_g[0], yb_x, w_out, name="out_proj0")
    meta_b = jnp.broadcast_to(meta[None], (nb, N_META, D_MODEL)).reshape(nb * N_META, D_MODEL)
    h1_m = _out_proj0(meta_b, yg_m, ss_m, e_ssd_norm_g[0], yb_m, w_out, name="out_proj0_meta")

    w_in1 = o_w_in[0].astype(BF16)
    p1_x = _in_proj(h1_x, o_norm_g[0], w_in1, name="in_proj1")
    p1_m = _in_proj(h1_m, o_norm_g[0], w_in1, name="in_proj1_meta")
    o_x = _natten(p1_x, p1_m, o_rpb[0])
    out = _out_proj1(h1_x, o_x, o_w_out[0].astype(BF16), final_norm_g, name="out_proj1")
    return out.reshape(nb, SEQ, D_MODEL)
```

```python
import functools

import jax
import jax.numpy as jnp
import numpy as np
from jax import lax
from jax.experimental import pallas as pl
from jax.experimental.pallas import tpu as pltpu

F32 = jnp.float32
BF16 = jnp.bfloat16

D_MODEL = 2048
BATCH = 4
SEQ = 2048
N_META = 16
GRID_W = 64
EPS = 1e-5

SSD_HEAD_DIM = 64
SSD_WIDTH = 2048
SSD_HEADS = 32
SSD_GROUPS = 8
SSD_STATE = 128
SSD_CONV = 5
CHUNK = 128
HEADS_PER_GROUP = SSD_HEADS // SSD_GROUPS
GROUP_W = HEADS_PER_GROUP * SSD_HEAD_DIM

CONV_WIDTH = 2048
CONV_KERNEL = 31
CONV_HALO = CONV_KERNEL // 2

NA_HEAD_DIM = 64
NA_WIDTH = 2048
NA_HEADS = 32
NA_KH = 8
NA_KW = 16
GRID_ROWS = SEQ // GRID_W

PAD_L = (-(N_META + SEQ)) % CHUNK
T_PAD = PAD_L + N_META + SEQ
N_CHUNKS = T_PAD // CHUNK
RAW_OFF = 8
SSD_CHUNKS_PER_TRIP = 2
NEG_BIG = -1e30
LOG2E = 1.4426950408889634

VMEM_LIMIT_BYTES = 56 * 1024 * 1024

COL_ZA = 0
COL_XS = 2048
COL_B = 4096
COL_C = 5120
COL_ZB = 6144
COL_GV = 8192
COL_GG = 10240
N_MAIN = 12288
DT_COLS = 128


def _cparams(sem):
    return pltpu.CompilerParams(dimension_semantics=sem, vmem_limit_bytes=VMEM_LIMIT_BYTES)


def _dot(a, b):
    return jnp.dot(a, b, preferred_element_type=F32)


def _dot_nt(a, b):
    return lax.dot_general(a, b, (((1,), (1,)), ((), ())), preferred_element_type=F32)


def _silu(x):
    return x * jax.nn.sigmoid(x)


def _softplus(x):
    return jnp.maximum(x, 0.0) + jnp.log1p(jnp.exp(-jnp.abs(x)))


def _split3(v):
    hi = v.astype(BF16)
    r1 = v - hi.astype(F32)
    mid = r1.astype(BF16)
    lo = (r1 - mid.astype(F32)).astype(BF16)
    return hi, mid, lo


W_PREP_ROWS = 1024


def _w_prep_kernel(w_ref, wdt_ref, o_ref, odt_ref):
    o_ref[...] = w_ref[...].astype(BF16)

    @pl.when(pl.program_id(0) == 0)
    def _():
        w = wdt_ref[...]
        hi = w.astype(BF16)
        lo = (w - hi.astype(F32)).astype(BF16)
        pad = jnp.zeros((DT_COLS - 2 * SSD_HEADS, w.shape[1]), BF16)
        odt_ref[...] = jnp.concatenate([hi, pad, lo, pad], axis=0)


def _w_prep(w_t):
    n_all, d = w_t.shape
    ndt = 2 * SSD_HEADS
    dt0 = COL_ZB
    tr = W_PREP_ROWS
    return pl.pallas_call(
        _w_prep_kernel,
        grid=(N_MAIN // tr,),
        in_specs=[
            pl.BlockSpec((pl.Element(tr), pl.Element(d)),
                         lambda j: ((j * (tr // ndt) + jnp.where(j >= dt0 // tr, 1, 0)) * ndt, 0)),
            pl.BlockSpec((ndt, d), lambda j: (dt0 // ndt, 0)),
        ],
        out_specs=[
            pl.BlockSpec((tr, d), lambda j: (j, 0)),
            pl.BlockSpec((2 * DT_COLS, d), lambda j: (0, 0)),
        ],
        out_shape=[
            jax.ShapeDtypeStruct((N_MAIN, d), BF16),
            jax.ShapeDtypeStruct((2 * DT_COLS, d), BF16),
        ],
        compiler_params=_cparams(("arbitrary",)),
        name="w_in_prep",
    )(w_t, w_t)


def _in_proj_kernel(x_ref, g_ref, w_ref, *rest, with_dt, w_is_nk):
    if with_dt:
        wdt_ref, o_ref, dt_ref, u_ref = rest
    else:
        o_ref, u_ref = rest

    @pl.when(pl.program_id(1) == 0)
    def _():
        x = x_ref[...]
        ms = jnp.mean(x * x, axis=-1, keepdims=True)
        y = x * lax.rsqrt(ms + EPS) * g_ref[...]
        u = y.astype(BF16)
        u_ref[...] = u
        if with_dt:
            u_lo = (y - u.astype(F32)).astype(BF16)
            w_hi = wdt_ref[:DT_COLS, :]
            w_lo = wdt_ref[DT_COLS:, :]
            dt_ref[...] = _dot_nt(u, w_hi) + _dot_nt(u_lo, w_hi) + _dot_nt(u, w_lo)

    mm = _dot_nt if w_is_nk else _dot
    o_ref[...] = mm(u_ref[...], w_ref[...]).astype(o_ref.dtype)


def _in_proj(x2d, g, w, wdt=None, *, w_is_nk=False, name):
    m, d = x2d.shape
    n = w.shape[0] if w_is_nk else w.shape[1]
    tm = min(m, 1024)
    tn = 1024
    with_dt = wdt is not None
    in_specs = [
        pl.BlockSpec((tm, d), lambda i, j: (i, 0)),
        pl.BlockSpec((1, d), lambda i, j: (0, 0)),
        pl.BlockSpec((tn, d), lambda i, j: (j, 0)) if w_is_nk else pl.BlockSpec((d, tn), lambda i, j: (0, j)),
    ]
    out_shape = [jax.ShapeDtypeStruct((m, n), BF16)]
    out_specs = [pl.BlockSpec((tm, tn), lambda i, j: (i, j))]
    args = [x2d, g.reshape(1, d), w]
    if with_dt:
        in_specs.append(pl.BlockSpec((2 * DT_COLS, d), lambda i, j: (0, 0)))
        out_shape.append(jax.ShapeDtypeStruct((m, DT_COLS), F32))
        out_specs.append(pl.BlockSpec((tm, DT_COLS), lambda i, j: (i, 0)))
        args.append(wdt)
    res = pl.pallas_call(
        functools.partial(_in_proj_kernel, with_dt=with_dt, w_is_nk=w_is_nk),
        grid=(m // tm, n // tn),
        in_specs=in_specs,
        out_specs=out_specs,
        out_shape=out_shape,
        scratch_shapes=[pltpu.VMEM((tm, d), BF16)],
        compiler_params=_cparams(("parallel", "arbitrary")),
        name=name,
    )(*args)
    return res if with_dt else res[0]


def _out_proj0_kernel(h_ref, ya_ref, ss_ref, ng_ref, yb_ref, w_ref, o_ref):
    r = lax.rsqrt(ss_ref[:, 0:1] * (1.0 / SSD_WIDTH) + EPS)
    yan = (ya_ref[...].astype(F32) * r * ng_ref[...]).astype(BF16)
    o_ref[...] = h_ref[...] + _dot(yan, w_ref[:SSD_WIDTH, :]) + _dot(yb_ref[...], w_ref[SSD_WIDTH:, :])


def _out_proj0(h2d, ya, ss, ng, yb, w_out, *, name):
    m, d = h2d.shape
    tm = min(m, 256)
    return pl.pallas_call(
        _out_proj0_kernel,
        grid=(m // tm,),
        in_specs=[
            pl.BlockSpec((tm, d), lambda i: (i, 0)),
            pl.BlockSpec((tm, SSD_WIDTH), lambda i: (i, 0)),
            pl.BlockSpec((tm, 128), lambda i: (i, 0)),
            pl.BlockSpec((1, SSD_WIDTH), lambda i: (0, 0)),
            pl.BlockSpec((tm, CONV_WIDTH), lambda i: (i, 0)),
            pl.BlockSpec((SSD_WIDTH + CONV_WIDTH, d), lambda i: (0, 0), pipeline_mode=pl.Buffered(1)),
        ],
        out_specs=pl.BlockSpec((tm, d), lambda i: (i, 0)),
        out_shape=jax.ShapeDtypeStruct((m, d), F32),
        compiler_params=_cparams(("parallel",)),
        name=name,
    )(h2d, ya, ss, ng.reshape(1, SSD_WIDTH), yb, w_out)


def _out_proj1_kernel(h_ref, o_ref, w_ref, g_ref, out_ref):
    hn = h_ref[...] + _dot(o_ref[...], w_ref[...])
    ms = jnp.mean(hn * hn, axis=-1, keepdims=True)
    out_ref[...] = hn * lax.rsqrt(ms + EPS) * g_ref[...]


def _out_proj1(h2d, o, w, g, *, name):
    m, d = h2d.shape
    tm = 512
    return pl.pallas_call(
        _out_proj1_kernel,
        grid=(m // tm,),
        in_specs=[
            pl.BlockSpec((tm, d), lambda i: (i, 0)),
            pl.BlockSpec((tm, NA_WIDTH), lambda i: (i, 0)),
            pl.BlockSpec((NA_WIDTH, d), lambda i: (0, 0)),
            pl.BlockSpec((1, d), lambda i: (0, 0)),
        ],
        out_specs=pl.BlockSpec((tm, d), lambda i: (i, 0)),
        out_shape=jax.ShapeDtypeStruct((m, d), F32),
        compiler_params=_cparams(("parallel",)),
        name=name,
    )(h2d, o, w, g.reshape(1, d))


def _ssd_kernel(za_x, xs_x, bm_x, cm_x, za_m, xs_m, bm_m, cm_m, dtt_ref,
                cwx, cwb, cwc, cbx, cbb, cbc, biast, alogt, dsk,
                yg_x, yg_m, ss_x, ss_m,
                raw_x, raw_b, raw_c, xc, bc, cc, ybuf_f, ybuf_b, csd_t, at, cs_t, st_f, st_b):
    g = pl.program_id(1)

    for raw, m_ref, x_ref in ((raw_x, xs_m, xs_x), (raw_b, bm_m, bm_x), (raw_c, cm_m, cm_x)):
        w = raw.shape[1]
        raw[0:RAW_OFF + PAD_L, :] = jnp.zeros((RAW_OFF + PAD_L, w), F32)
        raw[RAW_OFF + PAD_L:RAW_OFF + CHUNK, :] = m_ref[...].astype(F32)
        raw[RAW_OFF + CHUNK:RAW_OFF + T_PAD, :] = x_ref[...].astype(F32)
        raw[RAW_OFF + T_PAD:RAW_OFF + T_PAD + 8, :] = jnp.zeros((8, w), F32)

    row = lax.broadcasted_iota(jnp.int32, (CHUNK, 1), 0)

    def conv_chunk(c, carry):
        base = pl.multiple_of(c * CHUNK, CHUNK)
        keep = jnp.logical_or(c > 0, row >= PAD_L)
        for raw, cw, cb, dst in ((raw_x, cwx, cbx, xc), (raw_b, cwb, cbb, bc), (raw_c, cwc, cbc, cc)):
            win = raw[pl.ds(base, CHUNK + 16), :]
            acc = cb[...]
            for k in range(SSD_CONV):
                s0 = RAW_OFF - SSD_CONV // 2 + k
                acc = acc + win[s0:s0 + CHUNK, :] * cw[k:k + 1, :]
            v = jnp.where(keep, _silu(acc), 0.0)
            dst[pl.ds(base, CHUNK), :] = v.astype(dst.dtype)
        return carry

    lax.fori_loop(0, N_CHUNKS, conv_chunk, 0)

    a_t = -jnp.exp(alogt[0])
    dtv = _softplus(dtt_ref[0, 0] + biast[0])
    at[...] = dtv * a_t
    csd_t[...] = -jnp.log2(dtv)

    li = lax.broadcasted_iota(jnp.int32, (CHUNK, CHUNK), 0)
    si = lax.broadcasted_iota(jnp.int32, (CHUNK, CHUNK), 1)
    lane = lax.broadcasted_iota(jnp.int32, (1, CHUNK), 1)
    lane_lo = lane < SSD_HEAD_DIM
    head_sel = (jnp.where(lane_lo, 1.0, 0.0).astype(BF16), jnp.where(lane_lo, 0.0, 1.0).astype(BF16))
    masks = (li >= si, li <= si)
    ends = (CHUNK - 1, 0)

    a2 = jnp.concatenate([at[:, c * CHUNK:(c + 1) * CHUNK] for c in range(N_CHUNKS)]
                         + [jnp.zeros((8, CHUNK), F32)], axis=0)
    pieces = _split3(a2)
    t_f = jnp.where(li <= si, 1.0, 0.0).astype(BF16)
    t_b = jnp.where(li >= si, 1.0, 0.0).astype(BF16)
    cs_f = sum(_dot(p, t_f) for p in pieces)
    cs_b = sum(_dot(p, t_b) for p in pieces)
    is_fwd = lax.rem(lax.broadcasted_iota(jnp.int32, (8 * (N_CHUNKS + 1), 1), 0), 8) < HEADS_PER_GROUP
    cs2 = jnp.where(is_fwd, cs_f, cs_b) * LOG2E
    for c in range(N_CHUNKS):
        cols = slice(c * CHUNK, (c + 1) * CHUNK)
        cs_t[:, cols] = cs2[8 * c:8 * c + 8]
        csd_t[:, cols] = csd_t[:, cols] + cs2[8 * c:8 * c + 8]

    st_f[...] = jnp.zeros_like(st_f)
    st_b[...] = jnp.zeros_like(st_b)

    def chunk_inputs(c):
        base = c * CHUNK if isinstance(c, int) else pl.multiple_of(c * CHUNK, CHUNK)
        xb = xc[pl.ds(base, CHUNK), :].astype(BF16)
        bb = bc[pl.ds(base, CHUNK), :]
        cb_ = cc[pl.ds(base, CHUNK), :]
        csr8 = cs_t[:, pl.ds(base, CHUNK)]
        csd8 = csd_t[:, pl.ds(base, CHUNK)]
        cbm = _dot_nt(cb_, bb)
        bt = bb.astype(F32).T
        csc8 = csr8.T
        return base, xb, cb_.astype(F32), csr8, csd8, cbm, bt, csc8

    def operands(d, inp):
        _, xb, cf, csr8, csd8, cbm, bt, csc8 = inp
        out = []
        for pair in range(2):
            xp = xb[:, pair * CHUNK:(pair + 1) * CHUNK]
            per_head = []
            decs = []
            for e in range(2):
                hd = 4 * d + 2 * pair + e
                xk = xp * head_sel[e]
                csb = jnp.broadcast_to(csc8[:, hd:hd + 1], (CHUNK, CHUNK))
                csd = csd8[hd:hd + 1, :]
                lmat = jnp.exp2(jnp.where(masks[d], csb - csd, NEG_BIG))
                mh = (cbm * lmat).astype(BF16)
                csc = (cf * jnp.exp2(csb)).astype(BF16)
                wend = csr8[hd:hd + 1, ends[d]:ends[d] + 1]
                bs = (bt * jnp.exp2(wend - csd)).astype(BF16)
                per_head.append((mh, csc, bs, xk))
                decs.append(jnp.exp2(wend))
            (mh0, csc0, bs0, xk0), (mh1, csc1, bs1, xk1) = per_head
            lhs_y = jnp.concatenate([mh0, mh1, csc0, csc1], axis=1)
            lhs_s = jnp.concatenate([bs0, bs1], axis=1)
            x_cat = jnp.concatenate([xk0, xk1], axis=0)
            out.append((lhs_y, lhs_s, x_cat, jnp.where(lane_lo, decs[0], decs[1])))
        return out

    def outputs(state, ybuf, base, ops):
        new_state = []
        for pair, (lhs_y, lhs_s, x_cat, dec) in enumerate(ops):
            ps = slice(pair * CHUNK, (pair + 1) * CHUNK)
            sp = state[:, ps]
            spb = sp.astype(BF16)
            rhs_y = jnp.concatenate([x_cat, spb * head_sel[0], spb * head_sel[1]], axis=0)
            ybuf[pl.ds(base, CHUNK), ps] = _dot(lhs_y, rhs_y)
            new_state.append(sp * dec + _dot(lhs_s, x_cat))
        return jnp.concatenate(new_state, axis=1)

    def scan_trip(chunks_f, chunks_b):
        inps_f = [chunk_inputs(c) for c in chunks_f]
        inps_b = [chunk_inputs(c) for c in chunks_b]
        ops_f = [operands(0, inp) for inp in inps_f]
        ops_b = [operands(1, inp) for inp in inps_b]
        sf = st_f[...]
        sb = st_b[...]
        for inp_f, op_f, inp_b, op_b in zip(inps_f, ops_f, inps_b, ops_b):
            sf = outputs(sf, ybuf_f, inp_f[0], op_f)
            sb = outputs(sb, ybuf_b, inp_b[0], op_b)
        st_f[...] = sf
        st_b[...] = sb

    def scan_step(i, carry):
        first = i * SSD_CHUNKS_PER_TRIP
        scan_trip([first + u for u in range(SSD_CHUNKS_PER_TRIP)],
                  [N_CHUNKS - 1 - first - u for u in range(SSD_CHUNKS_PER_TRIP)])
        return carry

    n_trips = N_CHUNKS // SSD_CHUNKS_PER_TRIP
    lax.fori_loop(0, n_trips, scan_step, 0)
    for c in range(n_trips * SSD_CHUNKS_PER_TRIP, N_CHUNKS):
        scan_trip([c], [N_CHUNKS - 1 - c])

    @pl.when(g == 0)
    def _():
        ss_x[...] = jnp.zeros_like(ss_x)
        ss_m[...] = jnp.zeros_like(ss_m)

    def finish(i, carry):
        for u in range(2):
            base = pl.multiple_of((2 * i + u) * CHUNK, CHUNK)
            rows = pl.ds(CHUNK + base, CHUNK)
            y = ybuf_f[rows, :] + ybuf_b[rows, :] + dsk[...] * xc[rows, :]
            yg = y * _silu(za_x[pl.ds(base, CHUNK), :].astype(F32))
            yg_x[pl.ds(base, CHUNK), :] = yg.astype(yg_x.dtype)
            ss_x[pl.ds(base, CHUNK), :] = ss_x[pl.ds(base, CHUNK), :] + jnp.sum(yg * yg, axis=-1, keepdims=True)
        return carry

    lax.fori_loop(0, SEQ // (2 * CHUNK), finish, 0)

    y = ybuf_f[PAD_L:CHUNK, :] + ybuf_b[PAD_L:CHUNK, :] + dsk[...] * xc[PAD_L:CHUNK, :]
    yg = y * _silu(za_m[...].astype(F32))
    yg_m[...] = yg.astype(yg_m.dtype)
    ss_m[...] = ss_m[...] + jnp.sum(yg * yg, axis=-1, keepdims=True)


def _ssd(proj_x, proj_m, dtt, conv_w, conv_b, biast, alogt, dsk):
    gw, sn = GROUP_W, SSD_STATE
    nb = BATCH

    def xspec(width, col0):
        return pl.BlockSpec((SEQ, width), lambda b, g, c=col0 // width: (b, c + g))

    def mspec(width, col0):
        return pl.BlockSpec((N_META, width), lambda b, g, c=col0 // width: (0, c + g))

    def wspec(rows, width, col0):
        return pl.BlockSpec((rows, width), lambda b, g, c=col0 // width: (0, c + g))

    in_specs = [
        xspec(gw, COL_ZA), xspec(gw, COL_XS), xspec(sn, COL_B), xspec(sn, COL_C),
        mspec(gw, COL_ZA), mspec(gw, COL_XS), mspec(sn, COL_B), mspec(sn, COL_C),
        pl.BlockSpec((1, 1, 8, T_PAD), lambda b, g: (b, g, 0, 0)),
        wspec(SSD_CONV, gw, 0), wspec(SSD_CONV, sn, SSD_WIDTH), wspec(SSD_CONV, sn, SSD_WIDTH + SSD_GROUPS * sn),
        wspec(1, gw, 0), wspec(1, sn, SSD_WIDTH), wspec(1, sn, SSD_WIDTH + SSD_GROUPS * sn),
        pl.BlockSpec((1, 8, 1), lambda b, g: (g, 0, 0)),
        pl.BlockSpec((1, 8, 1), lambda b, g: (g, 0, 0)),
        pl.BlockSpec((1, gw), lambda b, g: (0, g)),
    ]
    out_shape = [
        jax.ShapeDtypeStruct((nb * SEQ, SSD_WIDTH), BF16),
        jax.ShapeDtypeStruct((nb * N_META, SSD_WIDTH), BF16),
        jax.ShapeDtypeStruct((nb * SEQ, 128), F32),
        jax.ShapeDtypeStruct((nb * N_META, 128), F32),
    ]
    out_specs = [
        pl.BlockSpec((SEQ, gw), lambda b, g: (b, g)),
        pl.BlockSpec((N_META, gw), lambda b, g: (b, g)),
        pl.BlockSpec((SEQ, 128), lambda b, g: (b, 0)),
        pl.BlockSpec((N_META, 128), lambda b, g: (b, 0)),
    ]
    raw_rows = RAW_OFF + T_PAD + 8
    scratch = [
        pltpu.VMEM((raw_rows, gw), F32), pltpu.VMEM((raw_rows, sn), F32), pltpu.VMEM((raw_rows, sn), F32),
        pltpu.VMEM((T_PAD, gw), F32), pltpu.VMEM((T_PAD, sn), BF16), pltpu.VMEM((T_PAD, sn), BF16),
        pltpu.VMEM((T_PAD, gw), F32), pltpu.VMEM((T_PAD, gw), F32),
        pltpu.VMEM((8, T_PAD), F32), pltpu.VMEM((8, T_PAD), F32), pltpu.VMEM((8, T_PAD), F32),
        pltpu.VMEM((sn, gw), F32), pltpu.VMEM((sn, gw), F32),
    ]
    return pl.pallas_call(
        _ssd_kernel,
        grid=(nb, SSD_GROUPS),
        in_specs=in_specs,
        out_specs=out_specs,
        out_shape=out_shape,
        scratch_shapes=scratch,
        compiler_params=_cparams(("parallel", "arbitrary")),
        name="ssd_mixer",
    )(proj_x, proj_x, proj_x, proj_x, proj_m, proj_m, proj_m, proj_m, dtt,
      conv_w, conv_w, conv_w, conv_b, conv_b, conv_b, biast, alogt, dsk)


CONV_TT = 512
CONV_RC = 64
CONV_GR = 16
CONV_GROUPS_PER_TRIP = 4
CONV_CC = 128
CONV_LW = 128
CONV_D0 = 1
CONV_SB = 64
CONV_SROWS = 576
CONV_UROWS = CONV_SROWS + 8


def _conv_kernel(gv, gg, zb, gv_p, gg_p, gv_n, gg_n, gv_m, gg_m, zb_m, dww, dwb, lng, lnb,
                 yb_x, yb_m, ubuf, sh, cbuf, cbuf_m, wrep):
    i = pl.program_id(1)
    nt = pl.num_programs(1)
    tt = CONV_TT

    def glu(v_ref, g_ref):
        return v_ref[...].astype(F32) * jax.nn.sigmoid(g_ref[...].astype(F32))

    ubuf[0:16, :] = jnp.zeros((16, CONV_WIDTH), F32)
    ubuf[48 + tt:, :] = jnp.zeros((CONV_UROWS - 48 - tt, CONV_WIDTH), F32)
    ubuf[32:32 + tt, :] = glu(gv, gg)

    @pl.when(i == 0)
    def _():
        ubuf[16:32, :] = glu(gv_m, gg_m)

    @pl.when(i > 0)
    def _():
        ubuf[16:32, :] = glu(gv_p, gg_p)

    @pl.when(i == nt - 1)
    def _():
        ubuf[32 + tt:48 + tt, :] = jnp.zeros((16, CONV_WIDTH), F32)

    @pl.when(i < nt - 1)
    def _():
        ubuf[32 + tt:48 + tt, :] = glu(gv_n, gg_n)

    def conv_rows(row0, nrows, cs):
        outs = []
        for l0 in range(0, cs.stop - cs.start, CONV_LW):
            ls = slice(l0, l0 + CONV_LW)
            gs = slice(cs.start + l0, cs.start + l0 + CONV_LW)
            acc = jnp.broadcast_to(dwb[:, gs], (nrows, CONV_LW))
            for r in range(8):
                qs = [q for q in range(5) if 0 <= 8 * q + r - CONV_D0 < CONV_KERNEL]
                span = pl.ds(row0 + 8 * qs[0], 8 * (qs[-1] - qs[0]) + nrows)
                blk = ubuf[span, gs] if r == 0 else sh[r - 1, span, ls]
                for q in qs:
                    k = 8 * q + r - CONV_D0
                    tap = blk[8 * (q - qs[0]):8 * (q - qs[0]) + nrows, :]
                    acc = acc + tap * pltpu.repeat(wrep[k, :, gs], nrows // 8, axis=0)
            outs.append(acc)
        return jnp.concatenate(outs, axis=1)

    def norm_gate(c_rows, z_rows):
        mu = jnp.mean(c_rows, axis=-1, keepdims=True)
        d = c_rows - mu
        var = jnp.mean(d * d, axis=-1, keepdims=True)
        yn = d * lax.rsqrt(var + EPS) * lng[...] + lnb[...]
        return _silu(yn) * _silu(z_rows)

    for k in range(CONV_KERNEL):
        wrep[k] = jnp.broadcast_to(dww[k:k + 1, :], (8, CONV_WIDTH))

    for cc in range(CONV_WIDTH // CONV_CC):
        cs = slice(cc * CONV_CC, (cc + 1) * CONV_CC)

        def shift_block(rb, carry, cs=cs):
            base = pl.multiple_of(rb * CONV_SB, CONV_SB)
            win = ubuf[pl.ds(base, CONV_SB + 8), cs]
            for r in range(1, 8):
                sh[r - 1, pl.ds(base, CONV_SB), :] = win[r:r + CONV_SB, :]
            return carry

        lax.fori_loop(0, CONV_SROWS // CONV_SB, shift_block, 0)

        def conv_chunk(rc, carry, cs=cs):
            base = pl.multiple_of(rc * CONV_RC, CONV_RC)
            cbuf[pl.ds(base, CONV_RC), cs] = conv_rows(base + 16, CONV_RC, cs)
            return carry

        lax.fori_loop(0, tt // CONV_RC, conv_chunk, 0)

        @pl.when(i == 0)
        def _(cs=cs):
            cbuf_m[:, cs] = conv_rows(0, N_META, cs)

    def gate_chunk(rc, carry):
        for grp in range(CONV_GROUPS_PER_TRIP):
            base = pl.multiple_of(rc * (CONV_GROUPS_PER_TRIP * CONV_GR) + grp * CONV_GR, CONV_GR)
            rows = pl.ds(base, CONV_GR)
            out = norm_gate(cbuf[rows, :], zb[rows, :].astype(F32))
            yb_x[rows, :] = out.astype(yb_x.dtype)
        return carry

    lax.fori_loop(0, tt // (CONV_GROUPS_PER_TRIP * CONV_GR), gate_chunk, 0)

    @pl.when(i == 0)
    def _():
        out = norm_gate(cbuf_m[...], zb_m[...].astype(F32))
        yb_m[...] = out.astype(yb_m.dtype)


def _conformer_conv(proj_x, proj_m, dw_w, dw_b, ln_g, ln_b):
    tt = CONV_TT
    nt = SEQ // tt
    w = CONV_WIDTH
    hb = tt // 16
    last_hb = BATCH * SEQ // 16 - 1

    def main(col0):
        return pl.BlockSpec((tt, w), lambda b, i, c=col0 // w: (b * nt + i, c))

    def prev(col0):
        return pl.BlockSpec((16, w), lambda b, i, c=col0 // w: (jnp.maximum((b * nt + i) * hb - 1, 0), c))

    def nxt(col0):
        return pl.BlockSpec((16, w), lambda b, i, c=col0 // w: (jnp.minimum((b * nt + i + 1) * hb, last_hb), c))

    def meta(col0):
        return pl.BlockSpec((N_META, w), lambda b, i, c=col0 // w: (0, c))

    vec = pl.BlockSpec((1, w), lambda b, i: (0, 0))
    return pl.pallas_call(
        _conv_kernel,
        grid=(BATCH, nt),
        in_specs=[
            main(COL_GV), main(COL_GG), main(COL_ZB),
            prev(COL_GV), prev(COL_GG), nxt(COL_GV), nxt(COL_GG),
            meta(COL_GV), meta(COL_GG), meta(COL_ZB),
            pl.BlockSpec((CONV_KERNEL, w), lambda b, i: (0, 0)), vec, vec, vec,
        ],
        out_specs=[
            pl.BlockSpec((tt, w), lambda b, i: (b * nt + i, 0)),
            pl.BlockSpec((N_META, w), lambda b, i: (b, 0)),
        ],
        out_shape=[
            jax.ShapeDtypeStruct((BATCH * SEQ, w), BF16),
            jax.ShapeDtypeStruct((BATCH * N_META, w), BF16),
        ],
        scratch_shapes=[pltpu.VMEM((CONV_UROWS, w), F32), pltpu.VMEM((7, CONV_SROWS, CONV_CC), F32),
                        pltpu.VMEM((tt, w), F32), pltpu.VMEM((N_META, w), F32),
                        pltpu.VMEM((CONV_KERNEL, 8, w), F32)],
        compiler_params=_cparams(("parallel", "arbitrary")),
        name="conformer_conv",
    )(proj_x, proj_x, proj_x, proj_x, proj_x, proj_x, proj_x, proj_m, proj_m, proj_m,
      dw_w, dw_b.reshape(1, w), ln_g.reshape(1, w), ln_b.reshape(1, w))


NA_WIN = NA_KH * GRID_W


NA_ROWS_PER_TRIP = 16
NA_LOOKAHEAD = 3
NA_TBL_W = 1024


def _natten_kernel(q_ref, k_ref, v_ref, z_ref, km_ref, vm_ref, rp_ref, o_ref, bias_ref):
    lane = lax.broadcasted_iota(jnp.int32, (1, 2 * NA_HEAD_DIM), 1)
    lane_lo = lane < NA_HEAD_DIM
    scale = NA_HEAD_DIM ** -0.5
    q_sel = (jnp.where(lane_lo, scale, 0.0).astype(BF16), jnp.where(lane_lo, 0.0, scale).astype(BF16))
    km = km_ref[...]
    vm = vm_ref[...]

    @pl.when(pl.program_id(1) == 0)
    def _():
        qcol = lax.broadcasted_iota(jnp.int32, (GRID_W, NA_TBL_W), 0)
        kcol = lax.rem(lax.broadcasted_iota(jnp.int32, (GRID_W, NA_TBL_W), 1), GRID_W)
        wstart = jnp.clip(qcol - NA_KW // 2, 0, GRID_W - NA_KW)
        in_window = jnp.logical_and(kcol >= wstart, kcol < wstart + NA_KW)
        for e in range(2):
            rows = jnp.broadcast_to(rp_ref[e], (GRID_W, NA_TBL_W))
            for par in range(2):
                shift = (NA_TBL_W - (NA_KW - 1) - par * GRID_W) % NA_TBL_W
                t = pltpu.roll(rows, shift, 1, stride=1, stride_axis=0)
                bias_ref[par, e * GRID_W:(e + 1) * GRID_W, :] = jnp.where(in_window, t, NEG_BIG)

    def scores(r):
        rs = jnp.clip(r - NA_KH // 2, 0, GRID_ROWS - NA_KH)
        cls = rs - r + NA_KH - 1
        par = lax.rem(cls, 2)
        boff = pl.multiple_of((cls - par) * GRID_W, 2 * GRID_W)
        qs = pl.multiple_of(r * GRID_W, GRID_W)
        ks = pl.multiple_of(rs * GRID_W, GRID_W)
        q = q_ref[pl.ds(qs, GRID_W), :]
        kw = k_ref[pl.ds(ks, NA_WIN), :]
        q2 = jnp.concatenate([q * q_sel[0], q * q_sel[1]], axis=0)
        s = _dot_nt(q2, kw) + bias_ref[par, :, pl.ds(boff, NA_WIN)]
        sm = _dot_nt(q2, km)
        return qs, ks, s, sm

    def softmax(s, sm):
        m = jnp.maximum(jnp.max(s, axis=-1, keepdims=True), jnp.max(sm, axis=-1, keepdims=True))
        p = jnp.exp(s - m)
        pm = jnp.exp(sm - m)
        den = jnp.sum(p, axis=-1, keepdims=True) + jnp.sum(pm, axis=-1, keepdims=True)
        return p.astype(BF16), pm.astype(BF16), den

    def finish(qs, ks, p, pm, den):
        o2 = (_dot(p, v_ref[pl.ds(ks, NA_WIN), :]) + _dot(pm, vm)) / den
        o = jnp.where(lane_lo, o2[:GRID_W], o2[GRID_W:])
        o = o * _silu(z_ref[pl.ds(qs, GRID_W), :].astype(F32))
        o_ref[pl.ds(qs, GRID_W), :] = o.astype(o_ref.dtype)

    def trip(i, carry):
        first = i * NA_ROWS_PER_TRIP
        pending = [scores(first + u) for u in range(NA_LOOKAHEAD)]
        for u in range(NA_ROWS_PER_TRIP):
            qs, ks, s, sm = pending.pop(0)
            p, pm, den = softmax(s, sm)
            if u + NA_LOOKAHEAD < NA_ROWS_PER_TRIP:
                pending.append(scores(first + u + NA_LOOKAHEAD))
            finish(qs, ks, p, pm, den)
        return carry

    lax.fori_loop(0, GRID_ROWS // NA_ROWS_PER_TRIP, trip, 0)


def _natten(proj_x, proj_m, rpb):
    pw = 2 * NA_HEAD_DIM
    npair = NA_HEADS // 2
    n_dr = 2 * NA_KH - 1
    rp = jnp.pad(rpb.astype(F32), ((0, 0), (0, 0), (0, GRID_W - (2 * NA_KW - 1))))
    rp = jnp.pad(rp.reshape(NA_HEADS, n_dr * GRID_W), ((0, 0), (0, NA_TBL_W - n_dr * GRID_W)))
    rp = rp.reshape(NA_HEADS, 1, NA_TBL_W)

    def xspec(col0):
        return pl.BlockSpec((SEQ, pw), lambda h, b, c=col0 // pw: (b, c + h))

    def mspec(col0):
        return pl.BlockSpec((N_META, pw), lambda h, b, c=col0 // pw: (b, c + h))

    return pl.pallas_call(
        _natten_kernel,
        grid=(npair, BATCH),
        in_specs=[
            xspec(0), xspec(NA_WIDTH), xspec(2 * NA_WIDTH), xspec(3 * NA_WIDTH),
            mspec(NA_WIDTH), mspec(2 * NA_WIDTH),
            pl.BlockSpec((2, 1, NA_TBL_W), lambda h, b: (h, 0, 0)),
        ],
        out_specs=pl.BlockSpec((SEQ, pw), lambda h, b: (b, h)),
        out_shape=jax.ShapeDtypeStruct((BATCH * SEQ, NA_WIDTH), BF16),
        scratch_shapes=[pltpu.VMEM((2, 2 * GRID_W, NA_TBL_W), F32)],
        compiler_params=_cparams(("arbitrary", "arbitrary")),
        name="natten",
    )(proj_x, proj_x, proj_x, proj_x, proj_m, proj_m, rp)


def kernel(x, meta_tokens, e_norm_g, e_w_in, e_conv_w, e_conv_b, e_dt_bias, e_A_log, e_D, e_ssd_norm_g, e_dw_w, e_dw_b, e_ln_g, e_ln_b, e_w_out, o_norm_g, o_w_in, o_rpb, o_w_out, final_norm_g):
    nb = x.shape[0]
    x2d = x.reshape(nb * SEQ, D_MODEL)
    meta = meta_tokens.astype(x.dtype)

    w_main, wdt = _w_prep(jnp.swapaxes(e_w_in, 1, 2)[0])

    proj_x, dt_x = _in_proj(x2d, e_norm_g[0], w_main, wdt, w_is_nk=True, name="in_proj0")
    proj_m, dt_m = _in_proj(meta, e_norm_g[0], w_main, wdt, w_is_nk=True, name="in_proj0_meta")

    def group_dt(a, rows):
        a = a[:, :2 * SSD_HEADS].reshape(-1, rows, 2, SSD_GROUPS, HEADS_PER_GROUP)
        return jnp.transpose(a, (0, 3, 2, 4, 1)).reshape(-1, SSD_GROUPS, 2 * HEADS_PER_GROUP, rows)

    dt_seq_t = jnp.concatenate([
        jnp.full((nb, SSD_GROUPS, 2 * HEADS_PER_GROUP, PAD_L), NEG_BIG, F32),
        jnp.broadcast_to(group_dt(dt_m, N_META), (nb, SSD_GROUPS, 2 * HEADS_PER_GROUP, N_META)),
        group_dt(dt_x, SEQ),
    ], axis=3)

    def group_vec(a):
        return jnp.transpose(a.reshape(2, SSD_GROUPS, HEADS_PER_GROUP), (1, 0, 2)).reshape(SSD_GROUPS, -1)

    bias_g = group_vec(e_dt_bias[0].astype(F32))
    alog_g = group_vec(e_A_log[0].astype(F32))
    dsk = jnp.repeat(e_D[0].astype(F32), SSD_HEAD_DIM).reshape(1, SSD_WIDTH)

    yg_x, yg_m, ss_x, ss_m = _ssd(
        proj_x, proj_m, dt_seq_t, e_conv_w[0], e_conv_b[0].reshape(1, -1),
        bias_g[:, :, None], alog_g[:, :, None], dsk)
    yb_x, yb_m = _conformer_conv(proj_x, proj_m, e_dw_w[0], e_dw_b[0], e_ln_g[0], e_ln_b[0])

    w_out = e_w_out[0].astype(BF16)
    h1_x = _out_proj0(x2d, yg_x, ss_x, e_ssd_norm_g[0], yb_x, w_out, name="out_proj0")
    meta_b = jnp.broadcast_to(meta[None], (nb, N_META, D_MODEL)).reshape(nb * N_META, D_MODEL)
    h1_m = _out_proj0(meta_b, yg_m, ss_m, e_ssd_norm_g[0], yb_m, w_out, name="out_proj0_meta")

    w_in1 = o_w_in[0].astype(BF16)
    p1_x = _in_proj(h1_x, o_norm_g[0], w_in1, name="in_proj1")
    p1_m = _in_proj(h1_m, o_norm_g[0], w_in1, name="in_proj1_meta")
    o_x = _natten(p1_x, p1_m, o_rpb[0])
    out = _out_proj1(h1_x, o_x, o_w_out[0].astype(BF16), final_norm_g, name="out_proj1")
    return out.reshape(nb, SEQ, D_MODEL)
```
